```python
import math
import jax
import jax.numpy as jnp
from jax import lax
import numpy as np

D_MODEL = 4096
BATCH = 1
SEQ = 16384
DEPTH = 2

CHUNK = 64
Q_BLOCK = 128
PLE_DIM = 256
RMS_EPS = 1e-6
NEG_INF = -1e30

DIFF_QK_DIM = 64
DIFF_V_DIM = 2 * DIFF_QK_DIM
DIFF_HEADS = (D_MODEL // 2) // DIFF_V_DIM
DIFF_WIDTH = DIFF_HEADS * DIFF_V_DIM

MLA_V_DIM = 128
MLA_NOPE_DIM = 128
MLA_ROPE_DIM = 64
MLA_HEADS = (D_MODEL // 2) // MLA_V_DIM
MLA_WIDTH = MLA_HEADS * MLA_V_DIM
MLA_Q_RANK = 1024
MLA_KV_RANK = 512
ROPE_BASE = 10000.0

MIX_WIDTH = DIFF_WIDTH + MLA_WIDTH

_DQ = DIFF_HEADS * 2 * DIFF_QK_DIM
_DV = DIFF_HEADS * DIFF_V_DIM
SPLIT_POINTS = [_DQ, 2 * _DQ, 2 * _DQ + _DV, 2 * _DQ + _DV + MLA_Q_RANK,
                2 * _DQ + _DV + MLA_Q_RANK + MLA_KV_RANK]
IN_COLS = SPLIT_POINTS[-1] + MLA_ROPE_DIM

N_GROUPS = 4
EXPERTS_PER_GROUP = 8
N_EXPERTS = N_GROUPS * EXPERTS_PER_GROUP
TOP_K = 2
D_EXPERT = 768
MOE_ROW_BLOCK = 128

kernel_name = "hybrid_diff_mla_hmoe_streaming"


def rmsnorm(x, g):
    xf = x.astype(jnp.float32)
    y = xf * lax.rsqrt(jnp.mean(xf * xf, axis=-1, keepdims=True) + RMS_EPS)
    return (y * g.astype(jnp.float32)).astype(x.dtype)


def rope(x, pos):
    dim = x.shape[-1]
    inv_freq = 1.0 / (ROPE_BASE ** (jnp.arange(0, dim, 2, dtype=jnp.float32) / dim))
    ang = pos.astype(jnp.float32)[:, None] * inv_freq[None, :]
    cos = jnp.cos(ang)[None, :, None, :]
    sin = jnp.sin(ang)[None, :, None, :]
    x1, x2 = jnp.split(x.astype(jnp.float32), 2, axis=-1)
    return jnp.concatenate([x1 * cos - x2 * sin, x2 * cos + x1 * sin], axis=-1).astype(x.dtype)


def alibi_slopes(n_heads):
    return 2.0 ** (-8.0 * jnp.arange(1, n_heads + 1, dtype=jnp.float32) / n_heads)


def block_attention(q, k, v, pos, slopes, scale):
    b, s, h, c, dk = q.shape
    nb = s // Q_BLOCK
    q_blocks = jnp.moveaxis(q.reshape(b, nb, Q_BLOCK, h, c, dk), 1, 0)
    pos_blocks = pos.reshape(nb, Q_BLOCK)
    key_chunk = pos // CHUNK

    def one_block(args):
        q_blk, p_blk = args
        scores = jnp.einsum('bqhcd,bkhcd->bhcqk', q_blk, k,
                            preferred_element_type=jnp.float32) * scale
        if slopes is not None:
            dist = jnp.abs(p_blk[:, None] - pos[None, :]).astype(jnp.float32)
            scores = scores - slopes[None, :, None, None, None] * dist
        allowed = key_chunk[None, :] <= (p_blk // CHUNK)[:, None]
        scores = jnp.where(allowed, scores, NEG_INF)
        probs = jax.nn.softmax(scores, axis=-1).astype(v.dtype)
        return jnp.einsum('bhcqk,bkhd->bqhcd', probs, v)

    out = lax.map(one_block, (q_blocks, pos_blocks))
    return jnp.moveaxis(out, 0, 1).reshape(b, s, h, c, v.shape[-1])


def diff_attention(q_in, k_in, v_in, lam_q1, lam_k1, lam_q2, lam_k2, norm_g, layer, pos):
    b, s, _ = q_in.shape
    q = q_in.reshape(b, s, DIFF_HEADS, 2, DIFF_QK_DIM)
    k = k_in.reshape(b, s, DIFF_HEADS, 2, DIFF_QK_DIM)
    v = v_in.reshape(b, s, DIFF_HEADS, DIFF_V_DIM)
    o = block_attention(q, k, v, pos, alibi_slopes(DIFF_HEADS), DIFF_QK_DIM ** -0.5)
    lam_init = 0.8 - 0.6 * math.exp(-0.3 * layer)
    f32 = jnp.float32
    lam = (jnp.exp(jnp.sum(lam_q1.astype(f32) * lam_k1.astype(f32)))
           - jnp.exp(jnp.sum(lam_q2.astype(f32) * lam_k2.astype(f32))) + lam_init)
    of = o.astype(f32)
    d = of[:, :, :, 0, :] - lam * of[:, :, :, 1, :]
    d = rmsnorm(d, norm_g) * (1.0 - lam_init)
    return d.reshape(b, s, DIFF_WIDTH).astype(v_in.dtype)


def mla_attention(c_q, c_kv, k_rope, qa_norm_g, w_uq, kva_norm_g, w_ukv, pos):
    b, s, _ = c_q.shape
    q = (rmsnorm(c_q, qa_norm_g) @ w_uq).reshape(b, s, MLA_HEADS, MLA_NOPE_DIM + MLA_ROPE_DIM)
    kv = (rmsnorm(c_kv, kva_norm_g) @ w_ukv).reshape(b, s, MLA_HEADS, MLA_NOPE_DIM + MLA_V_DIM)
    q_nope, q_pe = q[..., :MLA_NOPE_DIM], rope(q[..., MLA_NOPE_DIM:], pos)
    k_nope, v = kv[..., :MLA_NOPE_DIM], kv[..., MLA_NOPE_DIM:]
    k_pe = jnp.broadcast_to(rope(k_rope[:, :, None, :], pos), (b, s, MLA_HEADS, MLA_ROPE_DIM))
    q_full = jnp.concatenate([q_nope, q_pe], axis=-1)[:, :, :, None, :]
    k_full = jnp.concatenate([k_nope, k_pe], axis=-1)[:, :, :, None, :]
    o = block_attention(q_full, k_full, v, pos, None, (MLA_NOPE_DIM + MLA_ROPE_DIM) ** -0.5)
    return o.reshape(b, s, MLA_WIDTH)


def routed_experts(xt, eid, gate, w_gate, w_up, w_down):
    n, d = xt.shape
    a = n * TOP_K
    flat_e = eid.reshape(a)
    flat_tok = jnp.arange(a, dtype=jnp.int32) // TOP_K
    flat_gate = gate.reshape(a)
    order = jnp.argsort(flat_e, stable=True)
    e_sorted = flat_e[order]
    counts = jnp.bincount(flat_e, length=N_EXPERTS)
    padded = (counts + MOE_ROW_BLOCK - 1) // MOE_ROW_BLOCK * MOE_ROW_BLOCK
    pend = jnp.cumsum(padded)
    pstart = pend - padded
    cstart = jnp.cumsum(counts) - counts
    dest = pstart[e_sorted] + jnp.arange(a) - cstart[e_sorted]
    n_rows = (a + N_EXPERTS * (MOE_ROW_BLOCK - 1) + MOE_ROW_BLOCK - 1) // MOE_ROW_BLOCK * MOE_ROW_BLOCK
    n_blk = n_rows // MOE_ROW_BLOCK
    row_tok = jnp.zeros((n_rows,), jnp.int32).at[dest].set(flat_tok[order])
    row_gate = jnp.zeros((n_rows,), jnp.float32).at[dest].set(flat_gate[order].astype(jnp.float32))
    blk_e = jnp.minimum(jnp.searchsorted(pend, jnp.arange(n_blk) * MOE_ROW_BLOCK, side='right'),
                        N_EXPERTS - 1)

    def one_block(args):
        tok, g, e = args
        xb = xt[tok]
        hid = jax.nn.silu(xb @ w_gate[e]) * (xb @ w_up[e])
        y = hid @ w_down[e]
        return y * g[:, None].astype(y.dtype)

    yb = lax.map(one_block, (row_tok.reshape(n_blk, MOE_ROW_BLOCK),
                             row_gate.reshape(n_blk, MOE_ROW_BLOCK), blk_e))
    return jnp.zeros_like(xt).at[row_tok].add(yb.reshape(n_rows, d))


def hierarchical_moe(x, w_group, b_group, w_router, b_router, w_gate, w_up, w_down):
    b, s, d = x.shape
    xt = x.reshape(b * s, d)
    n = xt.shape[0]
    xf = xt.astype(jnp.float32)
    g_prob = jax.nn.softmax(xf @ w_group.astype(jnp.float32) + b_group.astype(jnp.float32), axis=-1)
    g_val, g_idx = lax.top_k(g_prob, 1)
    e_logit = (xf @ w_router.astype(jnp.float32) + b_router.astype(jnp.float32)).reshape(
        n, N_GROUPS, EXPERTS_PER_GROUP)
    e_logit = jnp.take_along_axis(e_logit, g_idx[:, :, None], axis=1)[:, 0]
    top_logit, top_idx = lax.top_k(e_logit, TOP_K)
    gate = jax.nn.softmax(top_logit, axis=-1) * g_val
    eid = g_idx * EXPERTS_PER_GROUP + top_idx
    return routed_experts(xt, eid, gate, w_gate, w_up, w_down).reshape(b, s, d)


def setup_inputs(seed: int = 0) -> dict:
    key = jax.random.key(seed)
    ks = jax.random.split(key, 26)
    f32 = jnp.float32

    def nrm(k, shape, scale):
        return jax.random.normal(k, shape, f32) * scale

    def gain(k, shape):
        return 1.0 + 0.02 * jax.random.normal(k, shape, f32)

    D = D_MODEL
    return {
        "x": nrm(ks[0], (BATCH, SEQ, D), 1.0),
        "p": nrm(ks[1], (DEPTH, BATCH, SEQ, PLE_DIM), 1.0),
        "ln_mix_g": gain(ks[2], (DEPTH, D)),
        "w_in": nrm(ks[3], (DEPTH, D, IN_COLS), D ** -0.5),
        "lam_q1": nrm(ks[4], (DEPTH, DIFF_QK_DIM), 0.1),
        "lam_k1": nrm(ks[5], (DEPTH, DIFF_QK_DIM), 0.1),
        "lam_q2": nrm(ks[6], (DEPTH, DIFF_QK_DIM), 0.1),
        "lam_k2": nrm(ks[7], (DEPTH, DIFF_QK_DIM), 0.1),
        "diff_norm_g": gain(ks[8], (DEPTH, DIFF_V_DIM)),
        "mla_qa_norm_g": gain(ks[9], (DEPTH, MLA_Q_RANK)),
        "w_uq": nrm(ks[10], (DEPTH, MLA_Q_RANK, MLA_HEADS * (MLA_NOPE_DIM + MLA_ROPE_DIM)), MLA_Q_RANK ** -0.5),
        "mla_kva_norm_g": gain(ks[11], (DEPTH, MLA_KV_RANK)),
        "w_ukv": nrm(ks[12], (DEPTH, MLA_KV_RANK, MLA_HEADS * (MLA_NOPE_DIM + MLA_V_DIM)), MLA_KV_RANK ** -0.5),
        "w_out": nrm(ks[13], (DEPTH, MIX_WIDTH, D), MIX_WIDTH ** -0.5),
        "ln_ffn_g": gain(ks[14], (DEPTH, D)),
        "w_group_router": nrm(ks[15], (DEPTH, D, N_GROUPS), D ** -0.5),
        "b_group_router": nrm(ks[16], (DEPTH, N_GROUPS), 0.01),
        "w_expert_router": nrm(ks[17], (DEPTH, D, N_EXPERTS), D ** -0.5),
        "b_expert_router": nrm(ks[18], (DEPTH, N_EXPERTS), 0.01),
        "w_exp_gate": nrm(ks[19], (DEPTH, N_EXPERTS, D, D_EXPERT), D ** -0.5),
        "w_exp_up": nrm(ks[20], (DEPTH, N_EXPERTS, D, D_EXPERT), D ** -0.5),
        "w_exp_down": nrm(ks[21], (DEPTH, N_EXPERTS, D_EXPERT, D), D_EXPERT ** -0.5),
        "ln_ple_g": gain(ks[22], (DEPTH, D)),
        "w_ple_gate": nrm(ks[23], (DEPTH, D, D), D ** -0.5),
        "w_ple_proj": nrm(ks[24], (DEPTH, PLE_DIM, D), PLE_DIM ** -0.5),
        "ln_final_g": gain(ks[25], (D,)),
    }


def reference(x, p, ln_mix_g, w_in, lam_q1, lam_k1, lam_q2, lam_k2, diff_norm_g,
              mla_qa_norm_g, w_uq, mla_kva_norm_g, w_ukv, w_out, ln_ffn_g,
              w_group_router, b_group_router, w_expert_router, b_expert_router,
              w_exp_gate, w_exp_up, w_exp_down, ln_ple_g, w_ple_gate, w_ple_proj,
              ln_final_g):
    pos = jnp.arange(x.shape[1], dtype=jnp.int32)
    h = x
    for i in range(DEPTH):
        hn = rmsnorm(h, ln_mix_g[i])
        proj = hn @ w_in[i]
        q_d, k_d, v_d, c_q, c_kv, k_rope = jnp.split(proj, SPLIT_POINTS, axis=-1)
        y_diff = diff_attention(q_d, k_d, v_d, lam_q1[i], lam_k1[i], lam_q2[i], lam_k2[i],
                                diff_norm_g[i], i, pos)
        y_mla = mla_attention(c_q, c_kv, k_rope, mla_qa_norm_g[i], w_uq[i],
                              mla_kva_norm_g[i], w_ukv[i], pos)
        h = h + jnp.concatenate([y_diff, y_mla], axis=-1) @ w_out[i]
        h = h + hierarchical_moe(rmsnorm(h, ln_ffn_g[i]), w_group_router[i], b_group_router[i],
                                 w_expert_router[i], b_expert_router[i],
                                 w_exp_gate[i], w_exp_up[i], w_exp_down[i])
        hn = rmsnorm(h, ln_ple_g[i])
        h = h + jax.nn.sigmoid(hn @ w_ple_gate[i]) * (p[i] @ w_ple_proj[i])
    return rmsnorm(h, ln_final_g)
```

```python
import functools
import math

import jax
import jax.numpy as jnp
from jax import lax
from jax.experimental import pallas as pl
from jax.experimental.pallas import tpu as pltpu

F32 = jnp.float32
BF16 = jnp.bfloat16

CHUNK = 64
RMS_EPS = 1e-6
NEG_INF = -1e30
DIFF_QK_DIM = 64
HEAD_V_DIM = 128
MLA_NOPE_DIM = 128
MLA_ROPE_DIM = 64
ROPE_BASE = 10000.0
N_GROUPS = 4
EXPERTS_PER_GROUP = 8
TOP_K = 2

LANES = 128
MLA_QK_PAD = 256
V7X_VMEM_LIMIT = 56 * 1024 * 1024

NORM_ROWS = 256
MM_BM = 1024
MM_BN = 512
ATTN_TQ = 512
ATTN_TK = 512
ROUTER_ROWS = 256
MOE_ROWS = 256
MOVE_ROWS = 256


def _pick(n, pref, mult=LANES):
    if n <= pref:
        return n
    best = None
    for c in range(mult, pref + 1, mult):
        if n % c == 0:
            best = c
    assert best is not None, (n, pref, mult)
    return best


def _params(sem):
    return pltpu.CompilerParams(dimension_semantics=sem, vmem_limit_bytes=V7X_VMEM_LIMIT)


def _rmsnorm_kernel(x_ref, g_ref, o_ref):
    x = x_ref[...].astype(F32)
    ms = jnp.mean(x * x, axis=-1, keepdims=True)
    o_ref[...] = (x * lax.rsqrt(ms + RMS_EPS) * g_ref[...]).astype(o_ref.dtype)


def rmsnorm(x, g, *, col_start=0, width=None, out_dtype=BF16):
    s = x.shape[0]
    width = x.shape[1] if width is None else width
    assert col_start % width == 0
    cb = col_start // width
    bm = _pick(s, NORM_ROWS, 8)
    return pl.pallas_call(
        _rmsnorm_kernel,
        grid=(s // bm,),
        in_specs=[pl.BlockSpec((bm, width), lambda i: (i, cb)),
                  pl.BlockSpec((1, width), lambda i: (0, 0))],
        out_specs=pl.BlockSpec((bm, width), lambda i: (i, 0)),
        out_shape=jax.ShapeDtypeStruct((s, width), out_dtype),
        compiler_params=_params(("arbitrary",)),
        name="rmsnorm",
    )(x, g.reshape(1, width).astype(F32))


def _rope_lanes(x, cos, sin):
    return x * cos + pltpu.roll(x, LANES // 2, axis=1) * sin


def _mm_kernel(*refs, mode, bn):
    if mode == "plain":
        x_ref, w_ref, o_ref = refs
    elif mode in ("rope_upper", "rope_all"):
        x_ref, w_ref, cos_ref, sin_ref, o_ref = refs
    elif mode == "residual":
        x_ref, w_ref, res_ref, o_ref = refs
    elif mode == "ple":
        x_ref, w_ref, res_ref, p_ref, wp_ref, o_ref = refs
    acc = jnp.dot(x_ref[...], w_ref[...], preferred_element_type=F32)
    if mode == "plain":
        o_ref[...] = acc.astype(o_ref.dtype)
    elif mode == "rope_all":
        o_ref[...] = _rope_lanes(acc, cos_ref[...], sin_ref[...]).astype(o_ref.dtype)
    elif mode == "rope_upper":
        cos = cos_ref[...]
        sin = sin_ref[...]
        for g in range(bn // MLA_QK_PAD):
            lo = g * MLA_QK_PAD
            o_ref[:, lo:lo + LANES] = acc[:, lo:lo + LANES].astype(o_ref.dtype)
            o_ref[:, lo + LANES:lo + 2 * LANES] = _rope_lanes(
                acc[:, lo + LANES:lo + 2 * LANES], cos, sin).astype(o_ref.dtype)
    elif mode == "residual":
        o_ref[...] = res_ref[...] + acc
    elif mode == "ple":
        emb = jnp.dot(p_ref[...].astype(BF16), wp_ref[...], preferred_element_type=F32)
        o_ref[...] = res_ref[...] + jax.nn.sigmoid(acc) * emb


def matmul(x, w, *, mode="plain", out_dtype=BF16, cos=None, sin=None, res=None, p=None, wp=None):
    m, k = x.shape
    n = w.shape[1]
    bm = _pick(m, MM_BM, 8)
    bn = _pick(n, MM_BN, MLA_QK_PAD if mode == "rope_upper" else LANES)
    in_specs = [pl.BlockSpec((bm, k), lambda i, j: (i, 0)),
                pl.BlockSpec((k, bn), lambda i, j: (0, j))]
    args = [x, w]
    if mode in ("rope_upper", "rope_all"):
        in_specs += [pl.BlockSpec((bm, LANES), lambda i, j: (i, 0))] * 2
        args += [cos, sin]
    if mode in ("residual", "ple"):
        in_specs.append(pl.BlockSpec((bm, bn), lambda i, j: (i, j)))
        args.append(res)
    if mode == "ple":
        kp = p.shape[1]
        in_specs += [pl.BlockSpec((bm, kp), lambda i, j: (i, 0)),
                     pl.BlockSpec((kp, bn), lambda i, j: (0, j))]
        args += [p, wp]
    return pl.pallas_call(
        functools.partial(_mm_kernel, mode=mode, bn=bn),
        grid=(m // bm, n // bn),
        in_specs=in_specs,
        out_specs=pl.BlockSpec((bm, bn), lambda i, j: (i, j)),
        out_shape=jax.ShapeDtypeStruct((m, n), out_dtype),
        compiler_params=_params(("arbitrary", "arbitrary")),
        name="matmul_" + mode,
    )(*args)


def _online_softmax_step(s, p_v, m_ref, l_ref, acc_ref, c):
    m_prev = m_ref[c]
    m_new = jnp.maximum(m_prev, jnp.max(s, axis=1, keepdims=True))
    alpha = jnp.exp(m_prev - m_new)
    prob = jnp.exp(s - m_new)
    pv = jnp.dot(prob.astype(BF16), p_v, preferred_element_type=F32)
    acc_ref[c] = alpha * acc_ref[c] + pv[:, :HEAD_V_DIM]
    l_ref[c] = alpha * l_ref[c] + pv[:, HEAD_V_DIM:HEAD_V_DIM + 1]
    m_ref[c] = m_new


def _chunk_mask(tq, tk, q0, k0):
    row = lax.broadcasted_iota(jnp.int32, (tq, tk), 0) + q0
    col = lax.broadcasted_iota(jnp.int32, (tq, tk), 1) + k0
    return (col // CHUNK) <= (row // CHUNK)


def _ones_column(tk):
    return (lax.broadcasted_iota(jnp.int32, (tk, LANES), 1) == 0).astype(BF16)


def _diff_attn_kernel(slopes_ref, q_ref, k_ref, v_ref, lam_ref, g_ref, o_ref,
                      acc_ref, m_ref, l_ref, *, tq, tk, lam_init):
    h = pl.program_id(0)
    q0 = pl.program_id(1) * tq
    slope = slopes_ref[h]
    q = q_ref[...] * jnp.asarray(DIFF_QK_DIM ** -0.5, BF16)
    lane = lax.broadcasted_iota(jnp.int32, q.shape, 1)
    zero = jnp.zeros_like(q)
    q_maps = (jnp.where(lane < DIFF_QK_DIM, q, zero), jnp.where(lane >= DIFF_QK_DIM, q, zero))
    acc_ref[...] = jnp.zeros_like(acc_ref)
    l_ref[...] = jnp.zeros_like(l_ref)
    m_ref[...] = jnp.full_like(m_ref, NEG_INF)
    rel = (lax.broadcasted_iota(jnp.int32, (tq, tk), 0)
           - lax.broadcasted_iota(jnp.int32, (tq, tk), 1)).astype(F32)
    ones_col = _ones_column(tk)

    def key_block(k0, bias, mask):
        k = k_ref[pl.ds(k0, tk), :]
        p_v = jnp.concatenate([v_ref[pl.ds(k0, tk), :], ones_col], axis=1)
        for c in range(2):
            s = lax.dot_general(q_maps[c], k, (((1,), (1,)), ((), ())), preferred_element_type=F32)
            s = s - bias
            if mask is not None:
                s = jnp.where(mask, s, NEG_INF)
            _online_softmax_step(s, p_v, m_ref, l_ref, acc_ref, c)

    n_full = q0 // tk

    def body(kb, carry):
        k0 = pl.multiple_of(kb * tk, tk)
        key_block(k0, slope * (rel + (q0 - k0).astype(F32)), None)
        return carry

    lax.fori_loop(0, n_full, body, 0)
    kd = pl.multiple_of(n_full * tk, tk)
    key_block(kd, slope * jnp.abs(rel + (q0 - kd).astype(F32)), _chunk_mask(tq, tk, q0, kd))

    lam_rows = lam_ref[...]
    lam = (jnp.exp(jnp.sum(lam_rows[0:1] * lam_rows[1:2], axis=1, keepdims=True))
           - jnp.exp(jnp.sum(lam_rows[2:3] * lam_rows[3:4], axis=1, keepdims=True)) + lam_init)
    d = acc_ref[0] / l_ref[0] - lam * (acc_ref[1] / l_ref[1])
    ms = jnp.mean(d * d, axis=-1, keepdims=True)
    y = d * lax.rsqrt(ms + RMS_EPS) * g_ref[...] * (1.0 - lam_init)
    o_ref[...] = y.astype(o_ref.dtype)


def diff_attention(proj, lam_rows, norm_g, layer, n_heads):
    s = proj.shape[0]
    tq = _pick(s, ATTN_TQ, CHUNK)
    tk = tq
    lam_init = 0.8 - 0.6 * math.exp(-0.3 * layer)
    slopes = 2.0 ** (-8.0 * jnp.arange(1, n_heads + 1, dtype=F32) / n_heads)
    grid_spec = pltpu.PrefetchScalarGridSpec(
        num_scalar_prefetch=1,
        grid=(n_heads, s // tq),
        in_specs=[pl.BlockSpec((tq, LANES), lambda h, i, sl: (i, h)),
                  pl.BlockSpec((s, LANES), lambda h, i, sl: (0, n_heads + h)),
                  pl.BlockSpec((s, LANES), lambda h, i, sl: (0, 2 * n_heads + h)),
                  pl.BlockSpec((4, DIFF_QK_DIM), lambda h, i, sl: (0, 0)),
                  pl.BlockSpec((1, HEAD_V_DIM), lambda h, i, sl: (0, 0))],
        out_specs=pl.BlockSpec((tq, HEAD_V_DIM), lambda h, i, sl: (i, h)),
        scratch_shapes=[pltpu.VMEM((2, tq, HEAD_V_DIM), F32),
                        pltpu.VMEM((2, tq, 1), F32),
                        pltpu.VMEM((2, tq, 1), F32)])
    return pl.pallas_call(
        functools.partial(_diff_attn_kernel, tq=tq, tk=tk, lam_init=lam_init),
        grid_spec=grid_spec,
        out_shape=jax.ShapeDtypeStruct((s, n_heads * HEAD_V_DIM), BF16),
        compiler_params=_params(("arbitrary", "arbitrary")),
        name="diff_attention",
    )(slopes, proj, proj, proj, lam_rows, norm_g.reshape(1, HEAD_V_DIM).astype(F32))


def _mla_attn_kernel(q_ref, kn_ref, kpe_ref, v_ref, o_ref, acc_ref, m_ref, l_ref, *, tq, tk):
    q0 = pl.program_id(1) * tq
    scale = (MLA_NOPE_DIM + MLA_ROPE_DIM) ** -0.5
    q = (q_ref[...].astype(F32) * scale).astype(BF16)
    acc_ref[...] = jnp.zeros_like(acc_ref)
    l_ref[...] = jnp.zeros_like(l_ref)
    m_ref[...] = jnp.full_like(m_ref, NEG_INF)
    ones_col = _ones_column(tk)

    def key_block(k0, mask):
        k = jnp.concatenate([kn_ref[pl.ds(k0, tk), :], kpe_ref[pl.ds(k0, tk), :]], axis=1)
        p_v = jnp.concatenate([v_ref[pl.ds(k0, tk), :], ones_col], axis=1)
        s = lax.dot_general(q, k, (((1,), (1,)), ((), ())), preferred_element_type=F32)
        if mask is not None:
            s = jnp.where(mask, s, NEG_INF)
        _online_softmax_step(s, p_v, m_ref, l_ref, acc_ref, 0)

    n_full = q0 // tk

    def body(kb, carry):
        key_block(pl.multiple_of(kb * tk, tk), None)
        return carry

    lax.fori_loop(0, n_full, body, 0)
    kd = pl.multiple_of(n_full * tk, tk)
    key_block(kd, _chunk_mask(tq, tk, q0, kd))
    o_ref[...] = (acc_ref[0] / l_ref[0]).astype(o_ref.dtype)


def mla_attention(q, kv, k_pe, n_heads):
    s = q.shape[0]
    tq = _pick(s, ATTN_TQ, CHUNK)
    tk = tq
    return pl.pallas_call(
        functools.partial(_mla_attn_kernel, tq=tq, tk=tk),
        grid=(n_heads, s // tq),
        in_specs=[pl.BlockSpec((tq, MLA_QK_PAD), lambda h, i: (i, h)),
                  pl.BlockSpec((s, LANES), lambda h, i: (0, 2 * h)),
                  pl.BlockSpec((s, LANES), lambda h, i: (0, 0)),
                  pl.BlockSpec((s, LANES), lambda h, i: (0, 2 * h + 1))],
        out_specs=pl.BlockSpec((tq, HEAD_V_DIM), lambda h, i: (i, h)),
        out_shape=jax.ShapeDtypeStruct((s, n_heads * HEAD_V_DIM), BF16),
        scratch_shapes=[pltpu.VMEM((1, tq, HEAD_V_DIM), F32),
                        pltpu.VMEM((1, tq, 1), F32),
                        pltpu.VMEM((1, tq, 1), F32)],
        compiler_params=_params(("arbitrary", "arbitrary")),
        name="mla_attention",
    )(q, kv, k_pe, kv)


def _router_kernel(h_ref, g_ref, w_ref, b_ref, xn_ref, meta_i_ref, meta_f_ref, cnt_ref, carry_ref,
                   *, n_exp):
    tb = h_ref.shape[0]

    @pl.when(pl.program_id(0) == 0)
    def _():
        carry_ref[...] = jnp.zeros_like(carry_ref)

    x = h_ref[...]
    ms = jnp.mean(x * x, axis=-1, keepdims=True)
    xn = x * lax.rsqrt(ms + RMS_EPS) * g_ref[...]
    xn_ref[...] = xn.astype(xn_ref.dtype)
    logits = jnp.dot(xn, w_ref[...], preferred_element_type=F32,
                     precision=lax.Precision.HIGHEST) + b_ref[...]
    lane = lax.broadcasted_iota(jnp.int32, logits.shape, 1)
    lane_f = lane.astype(F32)
    big = jnp.asarray(4.0 * LANES, F32)

    def first_lane(cond):
        return jnp.min(jnp.where(cond, lane_f, big), axis=1, keepdims=True)

    is_group = (lane >= n_exp) & (lane < n_exp + N_GROUPS)
    g_logit = jnp.where(is_group, logits, NEG_INF)
    g_max = jnp.max(g_logit, axis=1, keepdims=True)
    g_sum = jnp.sum(jnp.where(is_group, jnp.exp(g_logit - g_max), 0.0), axis=1, keepdims=True)
    g_val = 1.0 / g_sum
    g_idx = first_lane(is_group & (g_logit == g_max)) - n_exp
    in_group = (lane < n_exp) & ((lane // EXPERTS_PER_GROUP).astype(F32) == g_idx)
    e_logit = jnp.where(in_group, logits, NEG_INF)
    top1 = jnp.max(e_logit, axis=1, keepdims=True)
    idx1 = first_lane(in_group & (e_logit == top1))
    e_logit2 = jnp.where(lane_f == idx1, NEG_INF, e_logit)
    top2 = jnp.max(e_logit2, axis=1, keepdims=True)
    idx2 = first_lane(in_group & (lane_f != idx1) & (e_logit2 == top2))
    ratio = jnp.exp(top2 - top1)
    gate1 = g_val / (1.0 + ratio)
    gate2 = g_val * ratio / (1.0 + ratio)

    sel1 = lane_f == idx1
    sel2 = lane_f == idx2
    onehot = jnp.where(sel1 | sel2, 1.0, 0.0)
    r = lax.broadcasted_iota(jnp.int32, (tb, tb), 0)
    c = lax.broadcasted_iota(jnp.int32, (tb, tb), 1)
    before = jnp.where(c < r, 1.0, 0.0).astype(BF16)
    pos = jnp.dot(before, onehot.astype(BF16), preferred_element_type=F32) + carry_ref[...]
    rank1 = jnp.sum(jnp.where(sel1, pos, 0.0), axis=1, keepdims=True)
    rank2 = jnp.sum(jnp.where(sel2, pos, 0.0), axis=1, keepdims=True)
    carry_ref[...] = carry_ref[...] + jnp.sum(onehot, axis=0, keepdims=True)
    cnt_ref[...] = carry_ref[...]

    def lanes4(a, b, c4, d):
        return jnp.where(lane == 0, a, jnp.where(lane == 1, b, jnp.where(lane == 2, c4, d)))

    meta_i_ref[...] = lanes4(idx1, idx2, rank1, rank2).astype(jnp.int32)
    meta_f_ref[...] = lanes4(gate1, gate2, 0.0, 0.0)


def moe_router(h, gain, w_group, b_group, w_expert, b_expert):
    s, d = h.shape
    n_exp = w_expert.shape[1]
    assert n_exp == N_GROUPS * EXPERTS_PER_GROUP and n_exp + N_GROUPS <= LANES
    pad = LANES - n_exp - N_GROUPS
    w_cat = jnp.concatenate([w_expert, w_group, jnp.zeros((d, pad), F32)], axis=1).astype(F32)
    b_cat = jnp.concatenate([b_expert, b_group, jnp.zeros((pad,), F32)]).reshape(1, LANES).astype(F32)
    tb = _pick(s, ROUTER_ROWS, 8)
    return pl.pallas_call(
        functools.partial(_router_kernel, n_exp=n_exp),
        grid=(s // tb,),
        in_specs=[pl.BlockSpec((tb, d), lambda i: (i, 0)),
                  pl.BlockSpec((1, d), lambda i: (0, 0)),
                  pl.BlockSpec((d, LANES), lambda i: (0, 0)),
                  pl.BlockSpec((1, LANES), lambda i: (0, 0))],
        out_specs=[pl.BlockSpec((tb, d), lambda i: (i, 0)),
                   pl.BlockSpec((tb, LANES), lambda i: (i, 0)),
                   pl.BlockSpec((tb, LANES), lambda i: (i, 0)),
                   pl.BlockSpec((1, LANES), lambda i: (0, 0))],
        out_shape=[jax.ShapeDtypeStruct((s, d), BF16),
                   jax.ShapeDtypeStruct((s, LANES), jnp.int32),
                   jax.ShapeDtypeStruct((s, LANES), F32),
                   jax.ShapeDtypeStruct((1, LANES), F32)],
        scratch_shapes=[pltpu.VMEM((1, LANES), F32)],
        compiler_params=_params(("arbitrary",)),
        name="moe_router",
    )(h, gain.reshape(1, d).astype(F32), w_cat, b_cat)


def _scatter_rows_kernel(d0_ref, d1_ref, x_ref, xs_init_ref, xs_ref, sems):
    del xs_init_ref
    tb = x_ref.shape[0]
    base = pl.program_id(0) * tb

    def copies(t):
        return (pltpu.make_async_copy(x_ref.at[t], xs_ref.at[d0_ref[base + t]], sems.at[0]),
                pltpu.make_async_copy(x_ref.at[t], xs_ref.at[d1_ref[base + t]], sems.at[1]))

    def start(t, carry):
        for cp in copies(t):
            cp.start()
        return carry

    def wait(t, carry):
        for cp in copies(t):
            cp.wait()
        return carry

    lax.fori_loop(0, tb, start, 0)
    lax.fori_loop(0, tb, wait, 0)


def scatter_rows(x3, dest0, dest1, n_rows):
    n, sub, _ = x3.shape
    tb = _pick(n, MOVE_ROWS, 8)
    grid_spec = pltpu.PrefetchScalarGridSpec(
        num_scalar_prefetch=2,
        grid=(n // tb,),
        in_specs=[pl.BlockSpec((tb, sub, LANES), lambda i, a, b: (i, 0, 0)),
                  pl.BlockSpec(memory_space=pl.ANY)],
        out_specs=pl.BlockSpec(memory_space=pl.ANY),
        scratch_shapes=[pltpu.SemaphoreType.DMA((2,))])
    return pl.pallas_call(
        _scatter_rows_kernel,
        grid_spec=grid_spec,
        out_shape=jax.ShapeDtypeStruct((n_rows, sub, LANES), x3.dtype),
        input_output_aliases={3: 0},
        compiler_params=_params(("arbitrary",)),
        name="moe_scatter_rows",
    )(dest0, dest1, x3, jnp.zeros((n_rows, sub, LANES), x3.dtype))


def _expert_up_kernel(be_ref, nv_ref, x_ref, wg_ref, wu_ref, o_ref):
    b = pl.program_id(0)

    @pl.when(b < nv_ref[0])
    def _():
        x = x_ref[...]
        gate = jnp.dot(x, wg_ref[...], preferred_element_type=F32)
        up = jnp.dot(x, wu_ref[...], preferred_element_type=F32)
        o_ref[...] = (jax.nn.silu(gate) * up).astype(o_ref.dtype)

    @pl.when(b >= nv_ref[0])
    def _():
        o_ref[...] = jnp.zeros_like(o_ref)


def _expert_down_kernel(be_ref, nv_ref, x_ref, wd_ref, o_ref):
    b = pl.program_id(0)

    @pl.when(b < nv_ref[0])
    def _():
        o_ref[...] = jnp.dot(x_ref[...], wd_ref[...], preferred_element_type=F32).astype(o_ref.dtype)

    @pl.when(b >= nv_ref[0])
    def _():
        o_ref[...] = jnp.zeros_like(o_ref)


def expert_ffn(xs, blk_e, n_valid, w_gate, w_up, w_down):
    n_rows, d = xs.shape
    de = w_gate.shape[2]
    rb = MOE_ROWS
    n_blk = n_rows // rb
    up_spec = pltpu.PrefetchScalarGridSpec(
        num_scalar_prefetch=2,
        grid=(n_blk,),
        in_specs=[pl.BlockSpec((rb, d), lambda b, be, nv: (b, 0)),
                  pl.BlockSpec((None, d, de), lambda b, be, nv: (be[b], 0, 0)),
                  pl.BlockSpec((None, d, de), lambda b, be, nv: (be[b], 0, 0))],
        out_specs=pl.BlockSpec((rb, de), lambda b, be, nv: (b, 0)))
    hid = pl.pallas_call(
        _expert_up_kernel,
        grid_spec=up_spec,
        out_shape=jax.ShapeDtypeStruct((n_rows, de), BF16),
        compiler_params=_params(("arbitrary",)),
        name="moe_expert_up",
    )(blk_e, n_valid, xs, w_gate, w_up)
    down_spec = pltpu.PrefetchScalarGridSpec(
        num_scalar_prefetch=2,
        grid=(n_blk,),
        in_specs=[pl.BlockSpec((rb, de), lambda b, be, nv: (b, 0)),
                  pl.BlockSpec((None, de, d), lambda b, be, nv: (be[b], 0, 0))],
        out_specs=pl.BlockSpec((rb, d), lambda b, be, nv: (b, 0)))
    return pl.pallas_call(
        _expert_down_kernel,
        grid_spec=down_spec,
        out_shape=jax.ShapeDtypeStruct((n_rows, d), BF16),
        compiler_params=_params(("arbitrary",)),
        name="moe_expert_down",
    )(blk_e, n_valid, hid, w_down)


def _combine_kernel(d0_ref, d1_ref, g0_ref, g1_ref, h_ref, y_ref, o_ref, buf_ref, sems):
    tb = h_ref.shape[0]
    base = pl.program_id(0) * tb

    def copies(t):
        return (pltpu.make_async_copy(y_ref.at[d0_ref[base + t]], buf_ref.at[0, t], sems.at[0]),
                pltpu.make_async_copy(y_ref.at[d1_ref[base + t]], buf_ref.at[1, t], sems.at[1]))

    def start(t, carry):
        for cp in copies(t):
            cp.start()
        return carry

    def wait(t, carry):
        for cp in copies(t):
            cp.wait()
        return carry

    def combine(t, carry):
        o_ref[t] = (h_ref[t] + g0_ref[base + t] * buf_ref[0, t].astype(F32)
                    + g1_ref[base + t] * buf_ref[1, t].astype(F32))
        return carry

    lax.fori_loop(0, tb, start, 0)
    lax.fori_loop(0, tb, wait, 0)
    lax.fori_loop(0, tb, combine, 0)


def combine_rows(h3, y3, dest0, dest1, gate0, gate1):
    n, sub, _ = h3.shape
    tb = _pick(n, MOVE_ROWS, 8)
    grid_spec = pltpu.PrefetchScalarGridSpec(
        num_scalar_prefetch=4,
        grid=(n // tb,),
        in_specs=[pl.BlockSpec((tb, sub, LANES), lambda i, a, b, c, d: (i, 0, 0)),
                  pl.BlockSpec(memory_space=pl.ANY)],
        out_specs=pl.BlockSpec((tb, sub, LANES), lambda i, a, b, c, d: (i, 0, 0)),
        scratch_shapes=[pltpu.VMEM((2, tb, sub, LANES), y3.dtype),
                        pltpu.SemaphoreType.DMA((2,))])
    return pl.pallas_call(
        _combine_kernel,
        grid_spec=grid_spec,
        out_shape=jax.ShapeDtypeStruct(h3.shape, F32),
        compiler_params=_params(("arbitrary",)),
        name="moe_combine_rows",
    )(dest0, dest1, gate0, gate1, h3, y3)


def hierarchical_moe(h, gain, w_group, b_group, w_expert, b_expert, w_gate, w_up, w_down):
    s, d = h.shape
    n_exp = w_expert.shape[1]
    sub = d // LANES
    rb = MOE_ROWS
    xn, meta_i, meta_f, cnt = moe_router(h, gain, w_group, b_group, w_expert, b_expert)
    counts = cnt[0, :n_exp].astype(jnp.int32)
    padded = (counts + rb - 1) // rb * rb
    pend = jnp.cumsum(padded)
    pstart = pend - padded
    eid = meta_i[:, :TOP_K]
    dest = pstart[eid] + meta_i[:, TOP_K:2 * TOP_K]
    n_rows = (s * TOP_K + n_exp * (rb - 1) + rb - 1) // rb * rb
    n_blk = n_rows // rb
    blk_e = jnp.minimum(jnp.searchsorted(pend, jnp.arange(n_blk, dtype=jnp.int32) * rb, side="right"),
                        n_exp - 1).astype(jnp.int32)
    n_valid = (pend[-1:] // rb).astype(jnp.int32)
    dest0, dest1 = dest[:, 0], dest[:, 1]
    xs3 = scatter_rows(xn.reshape(s, sub, LANES), dest0, dest1, n_rows)
    y = expert_ffn(xs3.reshape(n_rows, d), blk_e, n_valid, w_gate, w_up, w_down)
    out3 = combine_rows(h.reshape(s, sub, LANES), y.reshape(n_rows, sub, LANES),
                        dest0, dest1, meta_f[:, 0], meta_f[:, 1])
    return out3.reshape(s, d)


def _rope_tables(s):
    half = MLA_ROPE_DIM // 2
    inv_freq = 1.0 / (ROPE_BASE ** (jnp.arange(0, MLA_ROPE_DIM, 2, dtype=F32) / MLA_ROPE_DIM))
    ang = jnp.arange(s, dtype=F32)[:, None] * inv_freq[None, :]
    cos, sin = jnp.cos(ang), jnp.sin(ang)
    z = jnp.zeros((s, half), F32)
    return (jnp.concatenate([cos, z, cos, z], axis=1), jnp.concatenate([-sin, z, sin, z], axis=1))


def _spread_rope_cols(w):
    half = MLA_ROPE_DIM // 2
    z = jnp.zeros((w.shape[0], half), w.dtype)
    return jnp.concatenate([w[:, :half], z, w[:, half:], z], axis=1)


def _pad_uq(w_uq, n_heads):
    r = w_uq.shape[0]
    w = w_uq.reshape(r, n_heads, MLA_NOPE_DIM + MLA_ROPE_DIM)
    half = MLA_ROPE_DIM // 2
    z = jnp.zeros((r, n_heads, half), w.dtype)
    nope, pe = w[..., :MLA_NOPE_DIM], w[..., MLA_NOPE_DIM:]
    w = jnp.concatenate([nope, pe[..., :half], z, pe[..., half:], z], axis=-1)
    return w.reshape(r, n_heads * MLA_QK_PAD)


def kernel(x, p, ln_mix_g, w_in, lam_q1, lam_k1, lam_q2, lam_k2, diff_norm_g, mla_qa_norm_g, w_uq, mla_kva_norm_g, w_ukv, w_out, ln_ffn_g, w_group_router, b_group_router, w_expert_router, b_expert_router, w_exp_gate, w_exp_up, w_exp_down, ln_ple_g, w_ple_gate, w_ple_proj, ln_final_g):
    batch, s, d = x.shape
    assert batch == 1
    depth = w_in.shape[0]
    diff_width = d // 2
    n_diff = diff_width // HEAD_V_DIM
    n_mla = (d // 2) // HEAD_V_DIM
    q_rank = mla_qa_norm_g.shape[1]
    kv_rank = mla_kva_norm_g.shape[1]
    main_cols = 3 * diff_width + q_rank + kv_rank
    assert w_in.shape[2] == main_cols + MLA_ROPE_DIM
    cos_t, sin_t = _rope_tables(s)

    h = x.reshape(s, d)
    for i in range(depth):
        w_in_i = w_in[i]
        w_main = w_in_i[:, :main_cols].astype(BF16)
        w_rope = _spread_rope_cols(w_in_i[:, main_cols:]).astype(BF16)
        hn = rmsnorm(h, ln_mix_g[i])
        proj = matmul(hn, w_main)
        k_pe = matmul(hn, w_rope, mode="rope_all", cos=cos_t, sin=sin_t)
        lam_rows = jnp.stack([lam_q1[i], lam_k1[i], lam_q2[i], lam_k2[i]]).astype(F32)
        y_diff = diff_attention(proj, lam_rows, diff_norm_g[i], i, n_diff)
        cqn = rmsnorm(proj, mla_qa_norm_g[i], col_start=3 * diff_width, width=q_rank)
        ckvn = rmsnorm(proj, mla_kva_norm_g[i], col_start=3 * diff_width + q_rank, width=kv_rank)
        q_mla = matmul(cqn, _pad_uq(w_uq[i], n_mla).astype(BF16), mode="rope_upper", cos=cos_t, sin=sin_t)
        kv = matmul(ckvn, w_ukv[i].astype(BF16))
        y_mla = mla_attention(q_mla, kv, k_pe, n_mla)
        y = jnp.concatenate([y_diff, y_mla], axis=1)
        h = matmul(y, w_out[i].astype(BF16), mode="residual", out_dtype=F32, res=h)
        h = hierarchical_moe(h, ln_ffn_g[i], w_group_router[i], b_group_router[i],
                             w_expert_router[i], b_expert_router[i],
                             w_exp_gate[i].astype(BF16), w_exp_up[i].astype(BF16),
                             w_exp_down[i].astype(BF16))
        hn = rmsnorm(h, ln_ple_g[i])
        h = matmul(hn, w_ple_gate[i].astype(BF16), mode="ple", out_dtype=F32, res=h,
                   p=p[i].reshape(s, -1), wp=w_ple_proj[i].astype(BF16))
    return rmsnorm(h, ln_final_g, out_dtype=x.dtype).reshape(batch, s, d)
```

```python
import functools
import math

import jax
import jax.numpy as jnp
from jax import lax
from jax.experimental import pallas as pl
from jax.experimental.pallas import tpu as pltpu

F32 = jnp.float32
BF16 = jnp.bfloat16

CHUNK = 64
RMS_EPS = 1e-6
NEG_INF = -1e30
DIFF_QK_DIM = 64
HEAD_V_DIM = 128
MLA_NOPE_DIM = 128
MLA_ROPE_DIM = 64
ROPE_BASE = 10000.0
N_GROUPS = 4
EXPERTS_PER_GROUP = 8
TOP_K = 2
LOG2_E = 1.4426950408889634

LANES = 128
MLA_QK_PAD = 256
V7X_VMEM_LIMIT = 56 * 1024 * 1024

NORM_ROWS = 256
MM_BM = 1024
MM_BN = 512
ATTN_TILE = 512
KEY_LOOP_UNROLL = 4
ONES_ROWS = 16
ROUTER_ROWS = 256
MOE_ROWS = 256
MOVE_ROWS = 256

_NT = (((1,), (1,)), ((), ()))


def _pick(n, pref, mult=LANES):
    if n <= pref:
        return n
    best = None
    for c in range(mult, pref + 1, mult):
        if n % c == 0:
            best = c
    assert best is not None, (n, pref, mult)
    return best


def _params(sem):
    return pltpu.CompilerParams(dimension_semantics=sem, vmem_limit_bytes=V7X_VMEM_LIMIT)


def _rmsnorm_kernel(x_ref, g_ref, o_ref):
    x = x_ref[...].astype(F32)
    ms = jnp.mean(x * x, axis=-1, keepdims=True)
    o_ref[...] = (x * lax.rsqrt(ms + RMS_EPS) * g_ref[...]).astype(o_ref.dtype)


def rmsnorm(x, g, *, col_start=0, width=None, out_dtype=BF16):
    s = x.shape[0]
    width = x.shape[1] if width is None else width
    assert col_start % width == 0
    cb = col_start // width
    bm = _pick(s, NORM_ROWS, 8)
    return pl.pallas_call(
        _rmsnorm_kernel,
        grid=(s // bm,),
        in_specs=[pl.BlockSpec((bm, width), lambda i: (i, cb)),
                  pl.BlockSpec((1, width), lambda i: (0, 0))],
        out_specs=pl.BlockSpec((bm, width), lambda i: (i, 0)),
        out_shape=jax.ShapeDtypeStruct((s, width), out_dtype),
        compiler_params=_params(("arbitrary",)),
        name="rmsnorm",
    )(x, g.reshape(1, width).astype(F32))


def _rope_lanes(x, cos, sin):
    return x * cos + pltpu.roll(x, LANES // 2, axis=1) * sin


def _mm_kernel(*refs, mode, bn):
    if mode == "plain":
        x_ref, w_ref, o_ref = refs
    elif mode in ("rope_upper", "rope_all"):
        x_ref, w_ref, cos_ref, sin_ref, o_ref = refs
    elif mode == "residual":
        x_ref, w_ref, res_ref, o_ref = refs
    elif mode == "ple":
        x_ref, w_ref, res_ref, p_ref, wp_ref, o_ref = refs
    acc = jnp.dot(x_ref[...], w_ref[...], preferred_element_type=F32)
    if mode == "plain":
        o_ref[...] = acc.astype(o_ref.dtype)
    elif mode == "rope_all":
        o_ref[...] = _rope_lanes(acc, cos_ref[...], sin_ref[...]).astype(o_ref.dtype)
    elif mode == "rope_upper":
        cos = cos_ref[...]
        sin = sin_ref[...]
        for g in range(bn // MLA_QK_PAD):
            lo = g * MLA_QK_PAD
            o_ref[:, lo:lo + LANES] = acc[:, lo:lo + LANES].astype(o_ref.dtype)
            o_ref[:, lo + LANES:lo + 2 * LANES] = _rope_lanes(
                acc[:, lo + LANES:lo + 2 * LANES], cos, sin).astype(o_ref.dtype)
    elif mode == "residual":
        o_ref[...] = res_ref[...] + acc
    elif mode == "ple":
        emb = jnp.dot(p_ref[...].astype(BF16), wp_ref[...], preferred_element_type=F32)
        o_ref[...] = res_ref[...] + jax.nn.sigmoid(acc) * emb


def matmul(x, w, *, mode="plain", out_dtype=BF16, cos=None, sin=None, res=None, p=None, wp=None):
    m, k = x.shape
    n = w.shape[1]
    bm = _pick(m, MM_BM, 8)
    bn = _pick(n, MM_BN, MLA_QK_PAD if mode == "rope_upper" else LANES)
    in_specs = [pl.BlockSpec((bm, k), lambda i, j: (i, 0)),
                pl.BlockSpec((k, bn), lambda i, j: (0, j))]
    args = [x, w]
    if mode in ("rope_upper", "rope_all"):
        in_specs += [pl.BlockSpec((bm, LANES), lambda i, j: (i, 0))] * 2
        args += [cos, sin]
    if mode in ("residual", "ple"):
        in_specs.append(pl.BlockSpec((bm, bn), lambda i, j: (i, j)))
        args.append(res)
    if mode == "ple":
        kp = p.shape[1]
        in_specs += [pl.BlockSpec((bm, kp), lambda i, j: (i, 0)),
                     pl.BlockSpec((kp, bn), lambda i, j: (0, j))]
        args += [p, wp]
    return pl.pallas_call(
        functools.partial(_mm_kernel, mode=mode, bn=bn),
        grid=(m // bm, n // bn),
        in_specs=in_specs,
        out_specs=pl.BlockSpec((bm, bn), lambda i, j: (i, j)),
        out_shape=jax.ShapeDtypeStruct((m, n), out_dtype),
        compiler_params=_params(("arbitrary", "arbitrary")),
        name="matmul_" + mode,
    )(*args)


def _mm_nt_kernel(a_ref, b_ref, o_ref):
    o_ref[...] = lax.dot_general(a_ref[...], b_ref[...], _NT,
                                 preferred_element_type=F32).astype(o_ref.dtype)


def matmul_nt_blocked(a, b, bn):
    m, k = a.shape
    n = b.shape[0]
    bm = _pick(m, MM_BN, 8)
    return pl.pallas_call(
        _mm_nt_kernel,
        grid=(m // bm, n // bn),
        in_specs=[pl.BlockSpec((bm, k), lambda i, j: (i, 0)),
                  pl.BlockSpec((bn, k), lambda i, j: (j, 0))],
        out_specs=pl.BlockSpec((None, bm, bn), lambda i, j: (j, i, 0)),
        out_shape=jax.ShapeDtypeStruct((n // bn, m, bn), BF16),
        compiler_params=_params(("arbitrary", "arbitrary")),
        name="matmul_nt",
    )(a, b)


def _softmax_block(s_t, off, vt, m_ref, acc_ref, c):
    m_prev = m_ref[c]
    m_new = jnp.maximum(m_prev, jnp.max(s_t, axis=0, keepdims=True) - off)
    alpha = jnp.exp2(m_prev - m_new)
    p_t = jnp.exp2(s_t - (m_new + off))
    acc_ref[c] = alpha * acc_ref[c] + jnp.dot(vt, p_t.astype(BF16), preferred_element_type=F32)
    m_ref[c] = m_new


def _values_with_ones(vt_ref, kb, t):
    ones_rows = (lax.broadcasted_iota(jnp.int32, (ONES_ROWS, t), 0) == 0).astype(BF16)
    return jnp.concatenate([vt_ref[kb], ones_rows], axis=0)


def _pipelined_key_loop(n_full, scores, consume, s_a, s_b):
    scores(0, s_a)

    def two_blocks(kb):
        scores(kb + 1, s_b)
        consume(s_a, kb, False)
        scores(kb + 2, s_a)
        consume(s_b, kb + 1, False)

    def unrolled(j, carry):
        for u in range(0, KEY_LOOP_UNROLL, 2):
            two_blocks(KEY_LOOP_UNROLL * j + u)
        return carry

    n_main = n_full // KEY_LOOP_UNROLL
    lax.fori_loop(0, n_main, unrolled, 0)

    def pair(j, carry):
        two_blocks(2 * j)
        return carry

    lax.fori_loop(n_main * (KEY_LOOP_UNROLL // 2), n_full // 2, pair, 0)

    @pl.when(n_full % 2 == 1)
    def _():
        scores(n_full, s_b)
        consume(s_a, n_full - 1, False)
        consume(s_b, n_full, True)

    @pl.when(n_full % 2 == 0)
    def _():
        consume(s_a, n_full, True)


def _init_stats(acc_ref, m_ref):
    acc_ref[...] = jnp.zeros_like(acc_ref)
    m_ref[...] = jnp.full_like(m_ref, NEG_INF)


def _normalised(acc_ref, c):
    acc = acc_ref[c]
    return acc[:HEAD_V_DIM] / acc[HEAD_V_DIM:HEAD_V_DIM + 1]


def _chunk_mask_t(t):
    key = lax.broadcasted_iota(jnp.int32, (t, t), 0)
    qry = lax.broadcasted_iota(jnp.int32, (t, t), 1)
    return (key // CHUNK) <= (qry // CHUNK)


def _diff_attn_kernel(slopes_ref, q_ref, k_ref, vt_ref, lam_ref, g_ref, o_ref,
                      s_a, s_b, bias_ref, acc_ref, m_ref, *, t, lam_init):
    h = pl.program_id(0)
    qb = pl.program_id(1)
    slope = slopes_ref[h] * LOG2_E
    q = (q_ref[...].astype(F32) * (DIFF_QK_DIM ** -0.5 * LOG2_E)).astype(BF16)
    lane = lax.broadcasted_iota(jnp.int32, q.shape, 1)
    zero = jnp.zeros_like(q)
    q_maps = (jnp.where(lane < DIFF_QK_DIM, q, zero), jnp.where(lane >= DIFF_QK_DIM, q, zero))
    _init_stats(acc_ref, m_ref)
    rel = (lax.broadcasted_iota(jnp.int32, (t, t), 1)
           - lax.broadcasted_iota(jnp.int32, (t, t), 0)).astype(F32)
    bias_ref[...] = slope * rel

    def scores(kb, s_ref):
        k = k_ref[pl.ds(pl.multiple_of(kb * t, t), t), :]
        for c in range(2):
            s_ref[c] = lax.dot_general(k, q_maps[c], _NT, preferred_element_type=F32)

    def consume(s_ref, kb, diag):
        vt = _values_with_ones(vt_ref, kb, t)
        for c in range(2):
            if diag:
                s_t = jnp.where(_chunk_mask_t(t), s_ref[c] - jnp.abs(bias_ref[...]), NEG_INF)
                off = 0.0
            else:
                s_t = s_ref[c] - bias_ref[...]
                off = slope * ((qb - kb) * t).astype(F32)
            _softmax_block(s_t, off, vt, m_ref, acc_ref, c)

    _pipelined_key_loop(qb, scores, consume, s_a, s_b)

    lam_rows = lam_ref[...]
    lam = (jnp.exp(jnp.sum(lam_rows[0:1] * lam_rows[1:2], axis=1, keepdims=True))
           - jnp.exp(jnp.sum(lam_rows[2:3] * lam_rows[3:4], axis=1, keepdims=True)) + lam_init)
    d_t = _normalised(acc_ref, 0) - lam * _normalised(acc_ref, 1)
    ms = jnp.mean(d_t * d_t, axis=0, keepdims=True)
    y_t = d_t * lax.rsqrt(ms + RMS_EPS) * g_ref[...] * (1.0 - lam_init)
    o_ref[...] = y_t.T.astype(o_ref.dtype)


def diff_attention(proj, vt, lam_rows, norm_g, layer, n_heads):
    s = proj.shape[0]
    t = vt.shape[2]
    lam_init = 0.8 - 0.6 * math.exp(-0.3 * layer)
    slopes = 2.0 ** (-8.0 * jnp.arange(1, n_heads + 1, dtype=F32) / n_heads)
    grid_spec = pltpu.PrefetchScalarGridSpec(
        num_scalar_prefetch=1,
        grid=(n_heads, s // t),
        in_specs=[pl.BlockSpec((t, LANES), lambda h, i, sl: (i, h)),
                  pl.BlockSpec((s, LANES), lambda h, i, sl: (0, n_heads + h)),
                  pl.BlockSpec((s // t, HEAD_V_DIM, t), lambda h, i, sl: (0, h, 0)),
                  pl.BlockSpec((4, DIFF_QK_DIM), lambda h, i, sl: (0, 0)),
                  pl.BlockSpec((HEAD_V_DIM, 1), lambda h, i, sl: (0, 0))],
        out_specs=pl.BlockSpec((t, HEAD_V_DIM), lambda h, i, sl: (i, h)),
        scratch_shapes=[pltpu.VMEM((2, t, t), F32),
                        pltpu.VMEM((2, t, t), F32),
                        pltpu.VMEM((t, t), F32),
                        pltpu.VMEM((2, HEAD_V_DIM + ONES_ROWS, t), F32),
                        pltpu.VMEM((2, 1, t), F32)])
    return pl.pallas_call(
        functools.partial(_diff_attn_kernel, t=t, lam_init=lam_init),
        grid_spec=grid_spec,
        out_shape=jax.ShapeDtypeStruct((s, n_heads * HEAD_V_DIM), BF16),
        compiler_params=_params(("arbitrary", "arbitrary")),
        name="diff_attention",
    )(slopes, proj, proj, vt, lam_rows, norm_g.reshape(HEAD_V_DIM, 1).astype(F32))


def _mla_attn_kernel(q_ref, kn_ref, kpe_ref, vt_ref, o_ref, s_a, s_b, acc_ref, m_ref, *, t):
    qb = pl.program_id(1)
    scale = (MLA_NOPE_DIM + MLA_ROPE_DIM) ** -0.5 * LOG2_E
    q = (q_ref[...].astype(F32) * scale).astype(BF16)
    _init_stats(acc_ref, m_ref)

    def scores(kb, s_ref):
        k0 = pl.multiple_of(kb * t, t)
        k = jnp.concatenate([kn_ref[pl.ds(k0, t), :], kpe_ref[pl.ds(k0, t), :]], axis=1)
        s_ref[0] = lax.dot_general(k, q, _NT, preferred_element_type=F32)

    def consume(s_ref, kb, diag):
        s_t = s_ref[0]
        if diag:
            s_t = jnp.where(_chunk_mask_t(t), s_t, NEG_INF)
        _softmax_block(s_t, 0.0, _values_with_ones(vt_ref, kb, t), m_ref, acc_ref, 0)

    _pipelined_key_loop(qb, scores, consume, s_a, s_b)
    o_ref[...] = _normalised(acc_ref, 0).T.astype(o_ref.dtype)


def mla_attention(q, kn, k_pe, vt, n_heads):
    s = q.shape[0]
    t = vt.shape[2]
    return pl.pallas_call(
        functools.partial(_mla_attn_kernel, t=t),
        grid=(n_heads, s // t),
        in_specs=[pl.BlockSpec((t, MLA_QK_PAD), lambda h, i: (i, h)),
                  pl.BlockSpec((s, LANES), lambda h, i: (0, h)),
                  pl.BlockSpec((s, LANES), lambda h, i: (0, 0)),
                  pl.BlockSpec((s // t, HEAD_V_DIM, t), lambda h, i: (0, h, 0))],
        out_specs=pl.BlockSpec((t, HEAD_V_DIM), lambda h, i: (i, h)),
        out_shape=jax.ShapeDtypeStruct((s, n_heads * HEAD_V_DIM), BF16),
        scratch_shapes=[pltpu.VMEM((1, t, t), F32),
                        pltpu.VMEM((1, t, t), F32),
                        pltpu.VMEM((1, HEAD_V_DIM + ONES_ROWS, t), F32),
                        pltpu.VMEM((1, 1, t), F32)],
        compiler_params=_params(("arbitrary", "arbitrary")),
        name="mla_attention",
    )(q, kn, k_pe, vt)


def _router_kernel(h_ref, g_ref, w_ref, b_ref, xn_ref, meta_i_ref, meta_f_ref, cnt_ref, carry_ref,
                   *, n_exp):
    tb = h_ref.shape[0]

    @pl.when(pl.program_id(0) == 0)
    def _():
        carry_ref[...] = jnp.zeros_like(carry_ref)

    x = h_ref[...]
    ms = jnp.mean(x * x, axis=-1, keepdims=True)
    xn = x * lax.rsqrt(ms + RMS_EPS) * g_ref[...]
    xn_ref[...] = xn.astype(xn_ref.dtype)
    logits = jnp.dot(xn, w_ref[...], preferred_element_type=F32,
                     precision=lax.Precision.HIGHEST) + b_ref[...]
    lane = lax.broadcasted_iota(jnp.int32, logits.shape, 1)
    lane_f = lane.astype(F32)
    big = jnp.asarray(4.0 * LANES, F32)

    def first_lane(cond):
        return jnp.min(jnp.where(cond, lane_f, big), axis=1, keepdims=True)

    is_group = (lane >= n_exp) & (lane < n_exp + N_GROUPS)
    g_logit = jnp.where(is_group, logits, NEG_INF)
    g_max = jnp.max(g_logit, axis=1, keepdims=True)
    g_sum = jnp.sum(jnp.where(is_group, jnp.exp(g_logit - g_max), 0.0), axis=1, keepdims=True)
    g_val = 1.0 / g_sum
    g_idx = first_lane(is_group & (g_logit == g_max)) - n_exp
    in_group = (lane < n_exp) & ((lane // EXPERTS_PER_GROUP).astype(F32) == g_idx)
    e_logit = jnp.where(in_group, logits, NEG_INF)
    top1 = jnp.max(e_logit, axis=1, keepdims=True)
    idx1 = first_lane(in_group & (e_logit == top1))
    e_logit2 = jnp.where(lane_f == idx1, NEG_INF, e_logit)
    top2 = jnp.max(e_logit2, axis=1, keepdims=True)
    idx2 = first_lane(in_group & (lane_f != idx1) & (e_logit2 == top2))
    ratio = jnp.exp(top2 - top1)
    gate1 = g_val / (1.0 + ratio)
    gate2 = g_val * ratio / (1.0 + ratio)

    sel1 = lane_f == idx1
    sel2 = lane_f == idx2
    onehot = jnp.where(sel1 | sel2, 1.0, 0.0)
    r = lax.broadcasted_iota(jnp.int32, (tb, tb), 0)
    c = lax.broadcasted_iota(jnp.int32, (tb, tb), 1)
    before = jnp.where(c < r, 1.0, 0.0).astype(BF16)
    pos = jnp.dot(before, onehot.astype(BF16), preferred_element_type=F32) + carry_ref[...]
    rank1 = jnp.sum(jnp.where(sel1, pos, 0.0), axis=1, keepdims=True)
    rank2 = jnp.sum(jnp.where(sel2, pos, 0.0), axis=1, keepdims=True)
    carry_ref[...] = carry_ref[...] + jnp.sum(onehot, axis=0, keepdims=True)
    cnt_ref[...] = carry_ref[...]

    def lanes4(a, b, c4, d):
        return jnp.where(lane == 0, a, jnp.where(lane == 1, b, jnp.where(lane == 2, c4, d)))

    meta_i_ref[...] = lanes4(idx1, idx2, rank1, rank2).astype(jnp.int32)
    meta_f_ref[...] = lanes4(gate1, gate2, 0.0, 0.0)


def moe_router(h, gain, w_group, b_group, w_expert, b_expert):
    s, d = h.shape
    n_exp = w_expert.shape[1]
    assert n_exp == N_GROUPS * EXPERTS_PER_GROUP and n_exp + N_GROUPS <= LANES
    pad = LANES - n_exp - N_GROUPS
    w_cat = jnp.concatenate([w_expert, w_group, jnp.zeros((d, pad), F32)], axis=1).astype(F32)
    b_cat = jnp.concatenate([b_expert, b_group, jnp.zeros((pad,), F32)]).reshape(1, LANES).astype(F32)
    tb = _pick(s, ROUTER_ROWS, 8)
    return pl.pallas_call(
        functools.partial(_router_kernel, n_exp=n_exp),
        grid=(s // tb,),
        in_specs=[pl.BlockSpec((tb, d), lambda i: (i, 0)),
                  pl.BlockSpec((1, d), lambda i: (0, 0)),
                  pl.BlockSpec((d, LANES), lambda i: (0, 0)),
                  pl.BlockSpec((1, LANES), lambda i: (0, 0))],
        out_specs=[pl.BlockSpec((tb, d), lambda i: (i, 0)),
                   pl.BlockSpec((tb, LANES), lambda i: (i, 0)),
                   pl.BlockSpec((tb, LANES), lambda i: (i, 0)),
                   pl.BlockSpec((1, LANES), lambda i: (0, 0))],
        out_shape=[jax.ShapeDtypeStruct((s, d), BF16),
                   jax.ShapeDtypeStruct((s, LANES), jnp.int32),
                   jax.ShapeDtypeStruct((s, LANES), F32),
                   jax.ShapeDtypeStruct((1, LANES), F32)],
        scratch_shapes=[pltpu.VMEM((1, LANES), F32)],
        compiler_params=_params(("arbitrary",)),
        name="moe_router",
    )(h, gain.reshape(1, d).astype(F32), w_cat, b_cat)


def _scatter_rows_kernel(d0_ref, d1_ref, x_ref, xs_init_ref, xs_ref, sems):
    del xs_init_ref
    tb = x_ref.shape[0]
    base = pl.program_id(0) * tb

    def copies(t):
        return (pltpu.make_async_copy(x_ref.at[t], xs_ref.at[d0_ref[base + t]], sems.at[0]),
                pltpu.make_async_copy(x_ref.at[t], xs_ref.at[d1_ref[base + t]], sems.at[1]))

    def start(t, carry):
        for cp in copies(t):
            cp.start()
        return carry

    def wait(t, carry):
        for cp in copies(t):
            cp.wait()
        return carry

    lax.fori_loop(0, tb, start, 0)
    lax.fori_loop(0, tb, wait, 0)


def scatter_rows(x3, dest0, dest1, n_rows):
    n, sub, _ = x3.shape
    tb = _pick(n, MOVE_ROWS, 8)
    grid_spec = pltpu.PrefetchScalarGridSpec(
        num_scalar_prefetch=2,
        grid=(n // tb,),
        in_specs=[pl.BlockSpec((tb, sub, LANES), lambda i, a, b: (i, 0, 0)),
                  pl.BlockSpec(memory_space=pl.ANY)],
        out_specs=pl.BlockSpec(memory_space=pl.ANY),
        scratch_shapes=[pltpu.SemaphoreType.DMA((2,))])
    return pl.pallas_call(
        _scatter_rows_kernel,
        grid_spec=grid_spec,
        out_shape=jax.ShapeDtypeStruct((n_rows, sub, LANES), x3.dtype),
        input_output_aliases={3: 0},
        compiler_params=_params(("arbitrary",)),
        name="moe_scatter_rows",
    )(dest0, dest1, x3, jnp.zeros((n_rows, sub, LANES), x3.dtype))


def _expert_up_kernel(be_ref, nv_ref, x_ref, wg_ref, wu_ref, o_ref):
    b = pl.program_id(0)

    @pl.when(b < nv_ref[0])
    def _():
        x = x_ref[...]
        gate = jnp.dot(x, wg_ref[...], preferred_element_type=F32)
        up = jnp.dot(x, wu_ref[...], preferred_element_type=F32)
        o_ref[...] = (jax.nn.silu(gate) * up).astype(o_ref.dtype)

    @pl.when(b >= nv_ref[0])
    def _():
        o_ref[...] = jnp.zeros_like(o_ref)


def _expert_down_kernel(be_ref, nv_ref, x_ref, wd_ref, o_ref):
    b = pl.program_id(0)

    @pl.when(b < nv_ref[0])
    def _():
        o_ref[...] = jnp.dot(x_ref[...], wd_ref[...], preferred_element_type=F32).astype(o_ref.dtype)

    @pl.when(b >= nv_ref[0])
    def _():
        o_ref[...] = jnp.zeros_like(o_ref)


def expert_ffn(xs, blk_e, n_valid, w_gate, w_up, w_down):
    n_rows, d = xs.shape
    de = w_gate.shape[2]
    rb = MOE_ROWS
    n_blk = n_rows // rb
    up_spec = pltpu.PrefetchScalarGridSpec(
        num_scalar_prefetch=2,
        grid=(n_blk,),
        in_specs=[pl.BlockSpec((rb, d), lambda b, be, nv: (b, 0)),
                  pl.BlockSpec((None, d, de), lambda b, be, nv: (be[b], 0, 0)),
                  pl.BlockSpec((None, d, de), lambda b, be, nv: (be[b], 0, 0))],
        out_specs=pl.BlockSpec((rb, de), lambda b, be, nv: (b, 0)))
    hid = pl.pallas_call(
        _expert_up_kernel,
        grid_spec=up_spec,
        out_shape=jax.ShapeDtypeStruct((n_rows, de), BF16),
        compiler_params=_params(("arbitrary",)),
        name="moe_expert_up",
    )(blk_e, n_valid, xs, w_gate, w_up)
    down_spec = pltpu.PrefetchScalarGridSpec(
        num_scalar_prefetch=2,
        grid=(n_blk,),
        in_specs=[pl.BlockSpec((rb, de), lambda b, be, nv: (b, 0)),
                  pl.BlockSpec((None, de, d), lambda b, be, nv: (be[b], 0, 0))],
        out_specs=pl.BlockSpec((rb, d), lambda b, be, nv: (b, 0)))
    return pl.pallas_call(
        _expert_down_kernel,
        grid_spec=down_spec,
        out_shape=jax.ShapeDtypeStruct((n_rows, d), BF16),
        compiler_params=_params(("arbitrary",)),
        name="moe_expert_down",
    )(blk_e, n_valid, hid, w_down)


def _combine_kernel(d0_ref, d1_ref, g0_ref, g1_ref, h_ref, y_ref, o_ref, buf_ref, sems):
    tb = h_ref.shape[0]
    base = pl.program_id(0) * tb

    def copies(t):
        return (pltpu.make_async_copy(y_ref.at[d0_ref[base + t]], buf_ref.at[0, t], sems.at[0]),
                pltpu.make_async_copy(y_ref.at[d1_ref[base + t]], buf_ref.at[1, t], sems.at[1]))

    def start(t, carry):
        for cp in copies(t):
            cp.start()
        return carry

    def wait(t, carry):
        for cp in copies(t):
            cp.wait()
        return carry

    def combine(t, carry):
        o_ref[t] = (h_ref[t] + g0_ref[base + t] * buf_ref[0, t].astype(F32)
                    + g1_ref[base + t] * buf_ref[1, t].astype(F32))
        return carry

    lax.fori_loop(0, tb, start, 0)
    lax.fori_loop(0, tb, wait, 0)
    lax.fori_loop(0, tb, combine, 0)


def combine_rows(h3, y3, dest0, dest1, gate0, gate1):
    n, sub, _ = h3.shape
    tb = _pick(n, MOVE_ROWS, 8)
    grid_spec = pltpu.PrefetchScalarGridSpec(
        num_scalar_prefetch=4,
        grid=(n // tb,),
        in_specs=[pl.BlockSpec((tb, sub, LANES), lambda i, a, b, c, d: (i, 0, 0)),
                  pl.BlockSpec(memory_space=pl.ANY)],
        out_specs=pl.BlockSpec((tb, sub, LANES), lambda i, a, b, c, d: (i, 0, 0)),
        scratch_shapes=[pltpu.VMEM((2, tb, sub, LANES), y3.dtype),
                        pltpu.SemaphoreType.DMA((2,))])
    return pl.pallas_call(
        _combine_kernel,
        grid_spec=grid_spec,
        out_shape=jax.ShapeDtypeStruct(h3.shape, F32),
        compiler_params=_params(("arbitrary",)),
        name="moe_combine_rows",
    )(dest0, dest1, gate0, gate1, h3, y3)


def hierarchical_moe(h, gain, w_group, b_group, w_expert, b_expert, w_gate, w_up, w_down):
    s, d = h.shape
    n_exp = w_expert.shape[1]
    sub = d // LANES
    rb = MOE_ROWS
    xn, meta_i, meta_f, cnt = moe_router(h, gain, w_group, b_group, w_expert, b_expert)
    counts = cnt[0, :n_exp].astype(jnp.int32)
    padded = (counts + rb - 1) // rb * rb
    pend = jnp.cumsum(padded)
    pstart = pend - padded
    eid = meta_i[:, :TOP_K]
    dest = pstart[eid] + meta_i[:, TOP_K:2 * TOP_K]
    n_rows = (s * TOP_K + n_exp * (rb - 1) + rb - 1) // rb * rb
    n_blk = n_rows // rb
    blk_start = jnp.arange(n_blk, dtype=jnp.int32) * rb
    blk_e = jnp.minimum(jnp.sum((pend[None, :] <= blk_start[:, None]).astype(jnp.int32), axis=1),
                        n_exp - 1).astype(jnp.int32)
    n_valid = (pend[-1:] // rb).astype(jnp.int32)
    dest0, dest1 = dest[:, 0], dest[:, 1]
    xs3 = scatter_rows(xn.reshape(s, sub, LANES), dest0, dest1, n_rows)
    y = expert_ffn(xs3.reshape(n_rows, d), blk_e, n_valid, w_gate, w_up, w_down)
    out3 = combine_rows(h.reshape(s, sub, LANES), y.reshape(n_rows, sub, LANES),
                        dest0, dest1, meta_f[:, 0], meta_f[:, 1])
    return out3.reshape(s, d)


def _rope_tables(s):
    half = MLA_ROPE_DIM // 2
    inv_freq = 1.0 / (ROPE_BASE ** (jnp.arange(0, MLA_ROPE_DIM, 2, dtype=F32) / MLA_ROPE_DIM))
    ang = jnp.arange(s, dtype=F32)[:, None] * inv_freq[None, :]
    cos, sin = jnp.cos(ang), jnp.sin(ang)
    z = jnp.zeros((s, half), F32)
    return (jnp.concatenate([cos, z, cos, z], axis=1), jnp.concatenate([-sin, z, sin, z], axis=1))


def _spread_rope_cols(w):
    half = MLA_ROPE_DIM // 2
    z = jnp.zeros((w.shape[0], half), w.dtype)
    return jnp.concatenate([w[:, :half], z, w[:, half:], z], axis=1)


def _pad_uq(w_uq, n_heads):
    r = w_uq.shape[0]
    w = w_uq.reshape(r, n_heads, MLA_NOPE_DIM + MLA_ROPE_DIM)
    half = MLA_ROPE_DIM // 2
    z = jnp.zeros((r, n_heads, half), w.dtype)
    nope, pe = w[..., :MLA_NOPE_DIM], w[..., MLA_NOPE_DIM:]
    w = jnp.concatenate([nope, pe[..., :half], z, pe[..., half:], z], axis=-1)
    return w.reshape(r, n_heads * MLA_QK_PAD)


def kernel(x, p, ln_mix_g, w_in, lam_q1, lam_k1, lam_q2, lam_k2, diff_norm_g, mla_qa_norm_g, w_uq, mla_kva_norm_g, w_ukv, w_out, ln_ffn_g, w_group_router, b_group_router, w_expert_router, b_expert_router, w_exp_gate, w_exp_up, w_exp_down, ln_ple_g, w_ple_gate, w_ple_proj, ln_final_g):
    batch, s, d = x.shape
    assert batch == 1
    depth = w_in.shape[0]
    diff_width = d // 2
    n_diff = diff_width // HEAD_V_DIM
    n_mla = (d // 2) // HEAD_V_DIM
    q_rank = mla_qa_norm_g.shape[1]
    kv_rank = mla_kva_norm_g.shape[1]
    v_start = 2 * diff_width
    cq_start = 3 * diff_width
    rope_start = cq_start + q_rank + kv_rank
    assert w_in.shape[2] == rope_start + MLA_ROPE_DIM
    t = _pick(s, ATTN_TILE, CHUNK)
    cos_t, sin_t = _rope_tables(s)

    h = x.reshape(s, d)
    for i in range(depth):
        w_in_i = w_in[i]
        w_main = jnp.concatenate([w_in_i[:, :v_start], w_in_i[:, cq_start:rope_start]], axis=1).astype(BF16)
        w_v_t = w_in_i[:, v_start:cq_start].T.astype(BF16)
        w_rope = _spread_rope_cols(w_in_i[:, rope_start:]).astype(BF16)
        hn = rmsnorm(h, ln_mix_g[i])
        proj = matmul(hn, w_main)
        vt_diff = matmul_nt_blocked(w_v_t, hn, t)
        k_pe = matmul(hn, w_rope, mode="rope_all", cos=cos_t, sin=sin_t)
        lam_rows = jnp.stack([lam_q1[i], lam_k1[i], lam_q2[i], lam_k2[i]]).astype(F32)
        y_diff = diff_attention(proj, vt_diff, lam_rows, diff_norm_g[i], i, n_diff)
        cqn = rmsnorm(proj, mla_qa_norm_g[i], col_start=v_start, width=q_rank)
        ckvn = rmsnorm(proj, mla_kva_norm_g[i], col_start=v_start + q_rank, width=kv_rank)
        q_mla = matmul(cqn, _pad_uq(w_uq[i], n_mla).astype(BF16), mode="rope_upper", cos=cos_t, sin=sin_t)
        w_ukv_i = w_ukv[i].reshape(kv_rank, n_mla, MLA_NOPE_DIM + HEAD_V_DIM)
        w_kn = w_ukv_i[:, :, :MLA_NOPE_DIM].reshape(kv_rank, n_mla * MLA_NOPE_DIM).astype(BF16)
        w_vm_t = w_ukv_i[:, :, MLA_NOPE_DIM:].reshape(kv_rank, n_mla * HEAD_V_DIM).T.astype(BF16)
        kn = matmul(ckvn, w_kn)
        vt_mla = matmul_nt_blocked(w_vm_t, ckvn, t)
        y_mla = mla_attention(q_mla, kn, k_pe, vt_mla, n_mla)
        y = jnp.concatenate([y_diff, y_mla], axis=1)
        h = matmul(y, w_out[i].astype(BF16), mode="residual", out_dtype=F32, res=h)
        h = hierarchical_moe(h, ln_ffn_g[i], w_group_router[i], b_group_router[i],
                             w_expert_router[i], b_expert_router[i],
                             w_exp_gate[i].astype(BF16), w_exp_up[i].astype(BF16),
                             w_exp_down[i].astype(BF16))
        hn = rmsnorm(h, ln_ple_g[i])
        h = matmul(hn, w_ple_gate[i].astype(BF16), mode="ple", out_dtype=F32, res=h,
                   p=p[i].reshape(s, -1), wp=w_ple_proj[i].astype(BF16))
    return rmsnorm(h, ln_final_g, out_dtype=x.dtype).reshape(batch, s, d)
```

```python
import functools
import math

import jax
import jax.numpy as jnp
from jax import lax
from jax.experimental import pallas as pl
from jax.experimental.pallas import tpu as pltpu

F32 = jnp.float32
BF16 = jnp.bfloat16

CHUNK = 64
RMS_EPS = 1e-6
NEG_INF = -1e30
DIFF_QK_DIM = 64
HEAD_V_DIM = 128
MLA_NOPE_DIM = 128
MLA_ROPE_DIM = 64
ROPE_BASE = 10000.0
N_GROUPS = 4
EXPERTS_PER_GROUP = 8
TOP_K = 2
LOG2_E = 1.4426950408889634

LANES = 128
MLA_QK_PAD = 256
V7X_VMEM_LIMIT = 56 * 1024 * 1024

NORM_ROWS = 256
MM_BM = 1024
MM_BN = 512
ATTN_TILE = 512
KEY_LOOP_UNROLLS = (8, 4, 2)
ONES_ROWS = 16
ROUTER_ROWS = 256
MOE_ROWS = 256
MOVE_ROWS = 256

_NT = (((1,), (1,)), ((), ()))


def _pick(n, pref, mult=LANES):
    if n <= pref:
        return n
    best = None
    for c in range(mult, pref + 1, mult):
        if n % c == 0:
            best = c
    assert best is not None, (n, pref, mult)
    return best


def _params(sem):
    return pltpu.CompilerParams(dimension_semantics=sem, vmem_limit_bytes=V7X_VMEM_LIMIT)


def _rmsnorm_kernel(x_ref, g_ref, o_ref):
    x = x_ref[...].astype(F32)
    ms = jnp.mean(x * x, axis=-1, keepdims=True)
    o_ref[...] = (x * lax.rsqrt(ms + RMS_EPS) * g_ref[...]).astype(o_ref.dtype)


def rmsnorm(x, g, *, col_start=0, width=None, out_dtype=BF16):
    s = x.shape[0]
    width = x.shape[1] if width is None else width
    assert col_start % width == 0
    cb = col_start // width
    bm = _pick(s, NORM_ROWS, 8)
    return pl.pallas_call(
        _rmsnorm_kernel,
        grid=(s // bm,),
        in_specs=[pl.BlockSpec((bm, width), lambda i: (i, cb)),
                  pl.BlockSpec((1, width), lambda i: (0, 0))],
        out_specs=pl.BlockSpec((bm, width), lambda i: (i, 0)),
        out_shape=jax.ShapeDtypeStruct((s, width), out_dtype),
        compiler_params=_params(("arbitrary",)),
        name="rmsnorm",
    )(x, g.reshape(1, width).astype(F32))


def _rope_lanes(x, cos, sin):
    return x * cos + pltpu.roll(x, LANES // 2, axis=1) * sin


def _mm_kernel(*refs, mode, bn):
    if mode == "plain":
        x_ref, w_ref, o_ref = refs
    elif mode in ("rope_upper", "rope_all"):
        x_ref, w_ref, cos_ref, sin_ref, o_ref = refs
    elif mode == "residual":
        x_ref, w_ref, res_ref, o_ref = refs
    elif mode == "ple":
        x_ref, w_ref, res_ref, p_ref, wp_ref, o_ref = refs
    acc = jnp.dot(x_ref[...], w_ref[...], preferred_element_type=F32)
    if mode == "plain":
        o_ref[...] = acc.astype(o_ref.dtype)
    elif mode == "rope_all":
        o_ref[...] = _rope_lanes(acc, cos_ref[...], sin_ref[...]).astype(o_ref.dtype)
    elif mode == "rope_upper":
        cos = cos_ref[...]
        sin = sin_ref[...]
        for g in range(bn // MLA_QK_PAD):
            lo = g * MLA_QK_PAD
            o_ref[:, lo:lo + LANES] = acc[:, lo:lo + LANES].astype(o_ref.dtype)
            o_ref[:, lo + LANES:lo + 2 * LANES] = _rope_lanes(
                acc[:, lo + LANES:lo + 2 * LANES], cos, sin).astype(o_ref.dtype)
    elif mode == "residual":
        o_ref[...] = res_ref[...] + acc
    elif mode == "ple":
        emb = jnp.dot(p_ref[...].astype(BF16), wp_ref[...], preferred_element_type=F32)
        o_ref[...] = res_ref[...] + jax.nn.sigmoid(acc) * emb


def matmul(x, w, *, mode="plain", out_dtype=BF16, cos=None, sin=None, res=None, p=None, wp=None):
    m, k = x.shape
    n = w.shape[1]
    bm = _pick(m, MM_BM, 8)
    bn = _pick(n, MM_BN, MLA_QK_PAD if mode == "rope_upper" else LANES)
    in_specs = [pl.BlockSpec((bm, k), lambda i, j: (i, 0)),
                pl.BlockSpec((k, bn), lambda i, j: (0, j))]
    args = [x, w]
    if mode in ("rope_upper", "rope_all"):
        in_specs += [pl.BlockSpec((bm, LANES), lambda i, j: (i, 0))] * 2
        args += [cos, sin]
    if mode in ("residual", "ple"):
        in_specs.append(pl.BlockSpec((bm, bn), lambda i, j: (i, j)))
        args.append(res)
    if mode == "ple":
        kp = p.shape[1]
        in_specs += [pl.BlockSpec((bm, kp), lambda i, j: (i, 0)),
                     pl.BlockSpec((kp, bn), lambda i, j: (0, j))]
        args += [p, wp]
    return pl.pallas_call(
        functools.partial(_mm_kernel, mode=mode, bn=bn),
        grid=(m // bm, n // bn),
        in_specs=in_specs,
        out_specs=pl.BlockSpec((bm, bn), lambda i, j: (i, j)),
        out_shape=jax.ShapeDtypeStruct((m, n), out_dtype),
        compiler_params=_params(("arbitrary", "arbitrary")),
        name="matmul_" + mode,
    )(*args)


def _mm_nt_kernel(a_ref, b_ref, o_ref):
    o_ref[...] = lax.dot_general(a_ref[...], b_ref[...], _NT,
                                 preferred_element_type=F32).astype(o_ref.dtype)


def matmul_nt_blocked(a, b, bn):
    m, k = a.shape
    n = b.shape[0]
    bm = _pick(m, MM_BN, 8)
    return pl.pallas_call(
        _mm_nt_kernel,
        grid=(m // bm, n // bn),
        in_specs=[pl.BlockSpec((bm, k), lambda i, j: (i, 0)),
                  pl.BlockSpec((bn, k), lambda i, j: (j, 0))],
        out_specs=pl.BlockSpec((None, bm, bn), lambda i, j: (j, i, 0)),
        out_shape=jax.ShapeDtypeStruct((n // bn, m, bn), BF16),
        compiler_params=_params(("arbitrary", "arbitrary")),
        name="matmul_nt",
    )(a, b)


def _softmax_block(s_t, off, vt, m_ref, acc_ref, c):
    m_prev = m_ref[c]
    m_new = jnp.maximum(m_prev, jnp.max(s_t, axis=0, keepdims=True) - off)
    alpha = jnp.exp2(m_prev - m_new)
    p_t = jnp.exp2(s_t - (m_new + off))
    acc_ref[c] = alpha * acc_ref[c] + jnp.dot(vt, p_t.astype(BF16), preferred_element_type=F32)
    m_ref[c] = m_new


def _values_with_ones(vt_ref, kb, t):
    ones_rows = (lax.broadcasted_iota(jnp.int32, (ONES_ROWS, t), 0) == 0).astype(BF16)
    return jnp.concatenate([vt_ref[kb], ones_rows], axis=0)


def _pipelined_key_loop(n_full, scores, consume, s_a, s_b):
    scores(0, s_a)

    def two_blocks(kb):
        scores(kb + 1, s_b)
        consume(s_a, kb, False)
        scores(kb + 2, s_a)
        consume(s_b, kb + 1, False)

    done = 0
    for unroll in KEY_LOOP_UNROLLS:
        count = (n_full - done) // unroll

        def unrolled(j, carry, unroll=unroll, done=done):
            for u in range(0, unroll, 2):
                two_blocks(done + unroll * j + u)
            return carry

        lax.fori_loop(0, count, unrolled, 0)
        done = done + unroll * count

    @pl.when(n_full % 2 == 1)
    def _():
        scores(n_full, s_b)
        consume(s_a, n_full - 1, False)
        consume(s_b, n_full, True)

    @pl.when(n_full % 2 == 0)
    def _():
        consume(s_a, n_full, True)


def _init_stats(acc_ref, m_ref):
    acc_ref[...] = jnp.zeros_like(acc_ref)
    m_ref[...] = jnp.full_like(m_ref, NEG_INF)


def _normalised(acc_ref, c):
    acc = acc_ref[c]
    return acc[:HEAD_V_DIM] / acc[HEAD_V_DIM:HEAD_V_DIM + 1]


def _split3(x):
    hi = x.astype(BF16).astype(F32)
    r1 = x - hi
    mid = r1.astype(BF16).astype(F32)
    lo = (r1 - mid).astype(BF16).astype(F32)
    return hi, mid, lo


def _by_lane(lane, columns):
    out = jnp.zeros(lane.shape, F32)
    for i, col in enumerate(columns):
        out = jnp.where(lane == i, col, out)
    return out.astype(BF16)


def _chunk_mask_t(t):
    key = lax.broadcasted_iota(jnp.int32, (t, t), 0)
    qry = lax.broadcasted_iota(jnp.int32, (t, t), 1)
    return (key // CHUNK) <= (qry // CHUNK)


def _diff_attn_kernel(slopes_ref, q_ref, k_ref, vt_ref, lam_ref, g_ref, o_ref,
                      s_a, s_b, bias_ref, acc_ref, m_ref, *, t, lam_init):
    h = pl.program_id(0)
    qb = pl.program_id(1)
    slope = slopes_ref[h] * LOG2_E
    q = (q_ref[...].astype(F32) * (DIFF_QK_DIM ** -0.5 * LOG2_E)).astype(BF16)
    lane = lax.broadcasted_iota(jnp.int32, q.shape, 1)
    zero = jnp.zeros_like(q)
    q_maps = (jnp.where(lane < DIFF_QK_DIM, q, zero), jnp.where(lane >= DIFF_QK_DIM, q, zero))
    _init_stats(acc_ref, m_ref)
    pos = lax.broadcasted_iota(jnp.int32, (t, LANES), 0)
    slope_v = jnp.full((t, LANES), slope, F32)
    s_hi, s_mid, s_lo = _split3(slope_v)
    a_hi, a_mid, a_lo = _split3(slope_v * pos.astype(F32))
    q_bias = _by_lane(lane, [s_hi, s_hi, s_mid, s_mid, s_lo, s_lo, -a_hi, -a_mid, -a_lo])
    one = jnp.ones((t, LANES), F32)
    j_even = (pos & ~1).astype(F32)
    j_odd = (pos & 1).astype(F32)
    k_bias = _by_lane(lane, [j_even, j_odd, j_even, j_odd, j_even, j_odd, one, one, one])
    q_aug = tuple(jnp.concatenate([qm, q_bias], axis=1) for qm in q_maps)
    rel = (lax.broadcasted_iota(jnp.int32, (t, t), 1)
           - lax.broadcasted_iota(jnp.int32, (t, t), 0)).astype(F32)
    bias_ref[...] = slope * rel

    def scores(kb, s_ref):
        k = jnp.concatenate([k_ref[pl.ds(pl.multiple_of(kb * t, t), t), :], k_bias], axis=1)
        for c in range(2):
            s_ref[c] = lax.dot_general(k, q_aug[c], _NT, preferred_element_type=F32)

    def consume(s_ref, kb, diag):
        vt = _values_with_ones(vt_ref, kb, t)
        for c in range(2):
            if diag:
                s_t = jnp.where(_chunk_mask_t(t), s_ref[c] + 2.0 * jnp.minimum(bias_ref[...], 0.0), NEG_INF)
                off = 0.0
            else:
                s_t = s_ref[c]
                off = slope * ((qb - kb) * t).astype(F32)
            _softmax_block(s_t, off, vt, m_ref, acc_ref, c)

    _pipelined_key_loop(qb, scores, consume, s_a, s_b)

    lam_rows = lam_ref[...]
    lam = (jnp.exp(jnp.sum(lam_rows[0:1] * lam_rows[1:2], axis=1, keepdims=True))
           - jnp.exp(jnp.sum(lam_rows[2:3] * lam_rows[3:4], axis=1, keepdims=True)) + lam_init)
    d_t = _normalised(acc_ref, 0) - lam * _normalised(acc_ref, 1)
    ms = jnp.mean(d_t * d_t, axis=0, keepdims=True)
    y_t = d_t * lax.rsqrt(ms + RMS_EPS) * g_ref[...] * (1.0 - lam_init)
    o_ref[...] = y_t.T.astype(o_ref.dtype)


def diff_attention(proj, vt, lam_rows, norm_g, layer, n_heads):
    s = proj.shape[0]
    t = vt.shape[2]
    lam_init = 0.8 - 0.6 * math.exp(-0.3 * layer)
    slopes = 2.0 ** (-8.0 * jnp.arange(1, n_heads + 1, dtype=F32) / n_heads)
    grid_spec = pltpu.PrefetchScalarGridSpec(
        num_scalar_prefetch=1,
        grid=(n_heads, s // t),
        in_specs=[pl.BlockSpec((t, LANES), lambda h, i, sl: (i, h)),
                  pl.BlockSpec((s, LANES), lambda h, i, sl: (0, n_heads + h)),
                  pl.BlockSpec((s // t, HEAD_V_DIM, t), lambda h, i, sl: (0, h, 0)),
                  pl.BlockSpec((4, DIFF_QK_DIM), lambda h, i, sl: (0, 0)),
                  pl.BlockSpec((HEAD_V_DIM, 1), lambda h, i, sl: (0, 0))],
        out_specs=pl.BlockSpec((t, HEAD_V_DIM), lambda h, i, sl: (i, h)),
        scratch_shapes=[pltpu.VMEM((2, t, t), F32),
                        pltpu.VMEM((2, t, t), F32),
                        pltpu.VMEM((t, t), F32),
                        pltpu.VMEM((2, HEAD_V_DIM + ONES_ROWS, t), F32),
                        pltpu.VMEM((2, 1, t), F32)])
    return pl.pallas_call(
        functools.partial(_diff_attn_kernel, t=t, lam_init=lam_init),
        grid_spec=grid_spec,
        out_shape=jax.ShapeDtypeStruct((s, n_heads * HEAD_V_DIM), BF16),
        compiler_params=_params(("arbitrary", "arbitrary")),
        name="diff_attention",
    )(slopes, proj, proj, vt, lam_rows, norm_g.reshape(HEAD_V_DIM, 1).astype(F32))


def _mla_attn_kernel(q_ref, kn_ref, kpe_ref, vt_ref, o_ref, s_a, s_b, acc_ref, m_ref, *, t):
    qb = pl.program_id(1)
    scale = (MLA_NOPE_DIM + MLA_ROPE_DIM) ** -0.5 * LOG2_E
    q = (q_ref[...].astype(F32) * scale).astype(BF16)
    _init_stats(acc_ref, m_ref)

    def scores(kb, s_ref):
        k0 = pl.multiple_of(kb * t, t)
        k = jnp.concatenate([kn_ref[pl.ds(k0, t), :], kpe_ref[pl.ds(k0, t), :]], axis=1)
        s_ref[0] = lax.dot_general(k, q, _NT, preferred_element_type=F32)

    def consume(s_ref, kb, diag):
        s_t = s_ref[0]
        if diag:
            s_t = jnp.where(_chunk_mask_t(t), s_t, NEG_INF)
        _softmax_block(s_t, 0.0, _values_with_ones(vt_ref, kb, t), m_ref, acc_ref, 0)

    _pipelined_key_loop(qb, scores, consume, s_a, s_b)
    o_ref[...] = _normalised(acc_ref, 0).T.astype(o_ref.dtype)


def mla_attention(q, kn, k_pe, vt, n_heads):
    s = q.shape[0]
    t = vt.shape[2]
    return pl.pallas_call(
        functools.partial(_mla_attn_kernel, t=t),
        grid=(n_heads, s // t),
        in_specs=[pl.BlockSpec((t, MLA_QK_PAD), lambda h, i: (i, h)),
                  pl.BlockSpec((s, LANES), lambda h, i: (0, h)),
                  pl.BlockSpec((s, LANES), lambda h, i: (0, 0)),
                  pl.BlockSpec((s // t, HEAD_V_DIM, t), lambda h, i: (0, h, 0))],
        out_specs=pl.BlockSpec((t, HEAD_V_DIM), lambda h, i: (i, h)),
        out_shape=jax.ShapeDtypeStruct((s, n_heads * HEAD_V_DIM), BF16),
        scratch_shapes=[pltpu.VMEM((1, t, t), F32),
                        pltpu.VMEM((1, t, t), F32),
                        pltpu.VMEM((1, HEAD_V_DIM + ONES_ROWS, t), F32),
                        pltpu.VMEM((1, 1, t), F32)],
        compiler_params=_params(("arbitrary", "arbitrary")),
        name="mla_attention",
    )(q, kn, k_pe, vt)


def _router_kernel(h_ref, g_ref, w_ref, b_ref, xn_ref, meta_i_ref, meta_f_ref, cnt_ref, carry_ref,
                   *, n_exp):
    tb = h_ref.shape[0]

    @pl.when(pl.program_id(0) == 0)
    def _():
        carry_ref[...] = jnp.zeros_like(carry_ref)

    x = h_ref[...]
    ms = jnp.mean(x * x, axis=-1, keepdims=True)
    xn = x * lax.rsqrt(ms + RMS_EPS) * g_ref[...]
    xn_ref[...] = xn.astype(xn_ref.dtype)
    logits = jnp.dot(xn, w_ref[...], preferred_element_type=F32,
                     precision=lax.Precision.HIGHEST) + b_ref[...]
    lane = lax.broadcasted_iota(jnp.int32, logits.shape, 1)
    lane_f = lane.astype(F32)
    big = jnp.asarray(4.0 * LANES, F32)

    def first_lane(cond):
        return jnp.min(jnp.where(cond, lane_f, big), axis=1, keepdims=True)

    is_group = (lane >= n_exp) & (lane < n_exp + N_GROUPS)
    g_logit = jnp.where(is_group, logits, NEG_INF)
    g_max = jnp.max(g_logit, axis=1, keepdims=True)
    g_sum = jnp.sum(jnp.where(is_group, jnp.exp(g_logit - g_max), 0.0), axis=1, keepdims=True)
    g_val = 1.0 / g_sum
    g_idx = first_lane(is_group & (g_logit == g_max)) - n_exp
    in_group = (lane < n_exp) & ((lane // EXPERTS_PER_GROUP).astype(F32) == g_idx)
    e_logit = jnp.where(in_group, logits, NEG_INF)
    top1 = jnp.max(e_logit, axis=1, keepdims=True)
    idx1 = first_lane(in_group & (e_logit == top1))
    e_logit2 = jnp.where(lane_f == idx1, NEG_INF, e_logit)
    top2 = jnp.max(e_logit2, axis=1, keepdims=True)
    idx2 = first_lane(in_group & (lane_f != idx1) & (e_logit2 == top2))
    ratio = jnp.exp(top2 - top1)
    gate1 = g_val / (1.0 + ratio)
    gate2 = g_val * ratio / (1.0 + ratio)

    sel1 = lane_f == idx1
    sel2 = lane_f == idx2
    onehot = jnp.where(sel1 | sel2, 1.0, 0.0)
    r = lax.broadcasted_iota(jnp.int32, (tb, tb), 0)
    c = lax.broadcasted_iota(jnp.int32, (tb, tb), 1)
    before = jnp.where(c < r, 1.0, 0.0).astype(BF16)
    pos = jnp.dot(before, onehot.astype(BF16), preferred_element_type=F32) + carry_ref[...]
    rank1 = jnp.sum(jnp.where(sel1, pos, 0.0), axis=1, keepdims=True)
    rank2 = jnp.sum(jnp.where(sel2, pos, 0.0), axis=1, keepdims=True)
    carry_ref[...] = carry_ref[...] + jnp.sum(onehot, axis=0, keepdims=True)
    cnt_ref[...] = carry_ref[...]

    def lanes4(a, b, c4, d):
        return jnp.where(lane == 0, a, jnp.where(lane == 1, b, jnp.where(lane == 2, c4, d)))

    meta_i_ref[...] = lanes4(idx1, idx2, rank1, rank2).astype(jnp.int32)
    meta_f_ref[...] = lanes4(gate1, gate2, 0.0, 0.0)


def moe_router(h, gain, w_group, b_group, w_expert, b_expert):
    s, d = h.shape
    n_exp = w_expert.shape[1]
    assert n_exp == N_GROUPS * EXPERTS_PER_GROUP and n_exp + N_GROUPS <= LANES
    pad = LANES - n_exp - N_GROUPS
    w_cat = jnp.concatenate([w_expert, w_group, jnp.zeros((d, pad), F32)], axis=1).astype(F32)
    b_cat = jnp.concatenate([b_expert, b_group, jnp.zeros((pad,), F32)]).reshape(1, LANES).astype(F32)
    tb = _pick(s, ROUTER_ROWS, 8)
    return pl.pallas_call(
        functools.partial(_router_kernel, n_exp=n_exp),
        grid=(s // tb,),
        in_specs=[pl.BlockSpec((tb, d), lambda i: (i, 0)),
                  pl.BlockSpec((1, d), lambda i: (0, 0)),
                  pl.BlockSpec((d, LANES), lambda i: (0, 0)),
                  pl.BlockSpec((1, LANES), lambda i: (0, 0))],
        out_specs=[pl.BlockSpec((tb, d), lambda i: (i, 0)),
                   pl.BlockSpec((tb, LANES), lambda i: (i, 0)),
                   pl.BlockSpec((tb, LANES), lambda i: (i, 0)),
                   pl.BlockSpec((1, LANES), lambda i: (0, 0))],
        out_shape=[jax.ShapeDtypeStruct((s, d), F32),
                   jax.ShapeDtypeStruct((s, LANES), jnp.int32),
                   jax.ShapeDtypeStruct((s, LANES), F32),
                   jax.ShapeDtypeStruct((1, LANES), F32)],
        scratch_shapes=[pltpu.VMEM((1, LANES), F32)],
        compiler_params=_params(("arbitrary",)),
        name="moe_router",
    )(h, gain.reshape(1, d).astype(F32), w_cat, b_cat)


def _scatter_rows_kernel(d0_ref, d1_ref, x_ref, xs_init_ref, xs_ref, sems):
    del xs_init_ref
    tb = x_ref.shape[0]
    base = pl.program_id(0) * tb

    def copies(t):
        row = x_ref.at[pl.ds(t, 1), :]
        return (pltpu.make_async_copy(row, xs_ref.at[pl.ds(d0_ref[base + t], 1), :], sems.at[0]),
                pltpu.make_async_copy(row, xs_ref.at[pl.ds(d1_ref[base + t], 1), :], sems.at[1]))

    def start(t, carry):
        for cp in copies(t):
            cp.start()
        return carry

    def wait(t, carry):
        for cp in copies(t):
            cp.wait()
        return carry

    lax.fori_loop(0, tb, start, 0)
    lax.fori_loop(0, tb, wait, 0)


def scatter_rows(x, dest0, dest1, n_rows):
    n, d = x.shape
    tb = _pick(n, MOVE_ROWS, 8)
    grid_spec = pltpu.PrefetchScalarGridSpec(
        num_scalar_prefetch=2,
        grid=(n // tb,),
        in_specs=[pl.BlockSpec((tb, d), lambda i, a, b: (i, 0)),
                  pl.BlockSpec(memory_space=pl.ANY)],
        out_specs=pl.BlockSpec(memory_space=pl.ANY),
        scratch_shapes=[pltpu.SemaphoreType.DMA((2,))])
    return pl.pallas_call(
        _scatter_rows_kernel,
        grid_spec=grid_spec,
        out_shape=jax.ShapeDtypeStruct((n_rows, d), x.dtype),
        input_output_aliases={3: 0},
        compiler_params=_params(("arbitrary",)),
        name="moe_scatter_rows",
    )(dest0, dest1, x, jnp.zeros((n_rows, d), x.dtype))


def _expert_up_kernel(be_ref, nv_ref, x_ref, wg_ref, wu_ref, o_ref):
    b = pl.program_id(0)

    @pl.when(b < nv_ref[0])
    def _():
        x = x_ref[...].astype(BF16)
        gate = jnp.dot(x, wg_ref[...], preferred_element_type=F32)
        up = jnp.dot(x, wu_ref[...], preferred_element_type=F32)
        o_ref[...] = (jax.nn.silu(gate) * up).astype(o_ref.dtype)

    @pl.when(b >= nv_ref[0])
    def _():
        o_ref[...] = jnp.zeros_like(o_ref)


def _expert_down_kernel(be_ref, nv_ref, x_ref, wd_ref, o_ref):
    b = pl.program_id(0)

    @pl.when(b < nv_ref[0])
    def _():
        o_ref[...] = jnp.dot(x_ref[...], wd_ref[...], preferred_element_type=F32).astype(o_ref.dtype)

    @pl.when(b >= nv_ref[0])
    def _():
        o_ref[...] = jnp.zeros_like(o_ref)


def expert_ffn(xs, blk_e, n_valid, w_gate, w_up, w_down):
    n_rows, d = xs.shape
    de = w_gate.shape[2]
    rb = MOE_ROWS
    n_blk = n_rows // rb
    up_spec = pltpu.PrefetchScalarGridSpec(
        num_scalar_prefetch=2,
        grid=(n_blk,),
        in_specs=[pl.BlockSpec((rb, d), lambda b, be, nv: (b, 0)),
                  pl.BlockSpec((None, d, de), lambda b, be, nv: (be[b], 0, 0)),
                  pl.BlockSpec((None, d, de), lambda b, be, nv: (be[b], 0, 0))],
        out_specs=pl.BlockSpec((rb, de), lambda b, be, nv: (b, 0)))
    hid = pl.pallas_call(
        _expert_up_kernel,
        grid_spec=up_spec,
        out_shape=jax.ShapeDtypeStruct((n_rows, de), BF16),
        compiler_params=_params(("arbitrary",)),
        name="moe_expert_up",
    )(blk_e, n_valid, xs, w_gate, w_up)
    down_spec = pltpu.PrefetchScalarGridSpec(
        num_scalar_prefetch=2,
        grid=(n_blk,),
        in_specs=[pl.BlockSpec((rb, de), lambda b, be, nv: (b, 0)),
                  pl.BlockSpec((None, de, d), lambda b, be, nv: (be[b], 0, 0))],
        out_specs=pl.BlockSpec((rb, d), lambda b, be, nv: (b, 0)))
    return pl.pallas_call(
        _expert_down_kernel,
        grid_spec=down_spec,
        out_shape=jax.ShapeDtypeStruct((n_rows, d), F32),
        compiler_params=_params(("arbitrary",)),
        name="moe_expert_down",
    )(blk_e, n_valid, hid, w_down)


def _combine_kernel(d0_ref, d1_ref, h_ref, gate_ref, y_ref, o_ref, buf_ref, sems):
    tb = h_ref.shape[0]
    base = pl.program_id(0) * tb

    def copies(t):
        return (pltpu.make_async_copy(y_ref.at[pl.ds(d0_ref[base + t], 1), :],
                                      buf_ref.at[0, pl.ds(t, 1), :], sems.at[0]),
                pltpu.make_async_copy(y_ref.at[pl.ds(d1_ref[base + t], 1), :],
                                      buf_ref.at[1, pl.ds(t, 1), :], sems.at[1]))

    def start(t, carry):
        for cp in copies(t):
            cp.start()
        return carry

    def wait(t, carry):
        for cp in copies(t):
            cp.wait()
        return carry

    lax.fori_loop(0, tb, start, 0)
    lax.fori_loop(0, tb, wait, 0)
    gates = gate_ref[...]
    o_ref[...] = h_ref[...] + gates[:, 0:1] * buf_ref[0] + gates[:, 1:2] * buf_ref[1]


def combine_rows(h, y, dest0, dest1, gates):
    n, d = h.shape
    tb = _pick(n, MOVE_ROWS, 8)
    grid_spec = pltpu.PrefetchScalarGridSpec(
        num_scalar_prefetch=2,
        grid=(n // tb,),
        in_specs=[pl.BlockSpec((tb, d), lambda i, a, b: (i, 0)),
                  pl.BlockSpec((tb, LANES), lambda i, a, b: (i, 0)),
                  pl.BlockSpec(memory_space=pl.ANY)],
        out_specs=pl.BlockSpec((tb, d), lambda i, a, b: (i, 0)),
        scratch_shapes=[pltpu.VMEM((2, tb, d), y.dtype),
                        pltpu.SemaphoreType.DMA((2,))])
    return pl.pallas_call(
        _combine_kernel,
        grid_spec=grid_spec,
        out_shape=jax.ShapeDtypeStruct(h.shape, F32),
        compiler_params=_params(("arbitrary",)),
        name="moe_combine_rows",
    )(dest0, dest1, h, gates, y)


def hierarchical_moe(h, gain, w_group, b_group, w_expert, b_expert, w_gate, w_up, w_down):
    s, d = h.shape
    n_exp = w_expert.shape[1]
    rb = MOE_ROWS
    xn, meta_i, meta_f, cnt = moe_router(h, gain, w_group, b_group, w_expert, b_expert)
    counts = cnt[0, :n_exp].astype(jnp.int32)
    padded = (counts + rb - 1) // rb * rb
    pend = jnp.cumsum(padded)
    pstart = pend - padded
    eid = meta_i[:, :TOP_K]
    dest = pstart[eid] + meta_i[:, TOP_K:2 * TOP_K]
    n_rows = (s * TOP_K + n_exp * (rb - 1) + rb - 1) // rb * rb
    n_blk = n_rows // rb
    blk_start = jnp.arange(n_blk, dtype=jnp.int32) * rb
    blk_e = jnp.minimum(jnp.sum((pend[None, :] <= blk_start[:, None]).astype(jnp.int32), axis=1),
                        n_exp - 1).astype(jnp.int32)
    n_valid = (pend[-1:] // rb).astype(jnp.int32)
    dest0, dest1 = dest[:, 0], dest[:, 1]
    xs = scatter_rows(xn, dest0, dest1, n_rows)
    y = expert_ffn(xs, blk_e, n_valid, w_gate, w_up, w_down)
    return combine_rows(h, y, dest0, dest1, meta_f)


def _rope_tables(s):
    half = MLA_ROPE_DIM // 2
    inv_freq = 1.0 / (ROPE_BASE ** (jnp.arange(0, MLA_ROPE_DIM, 2, dtype=F32) / MLA_ROPE_DIM))
    ang = jnp.arange(s, dtype=F32)[:, None] * inv_freq[None, :]
    cos, sin = jnp.cos(ang), jnp.sin(ang)
    z = jnp.zeros((s, half), F32)
    return (jnp.concatenate([cos, z, cos, z], axis=1), jnp.concatenate([-sin, z, sin, z], axis=1))


def _spread_rope_cols(w):
    half = MLA_ROPE_DIM // 2
    z = jnp.zeros((w.shape[0], half), w.dtype)
    return jnp.concatenate([w[:, :half], z, w[:, half:], z], axis=1)


def _pad_uq(w_uq, n_heads):
    r = w_uq.shape[0]
    w = w_uq.reshape(r, n_heads, MLA_NOPE_DIM + MLA_ROPE_DIM)
    half = MLA_ROPE_DIM // 2
    z = jnp.zeros((r, n_heads, half), w.dtype)
    nope, pe = w[..., :MLA_NOPE_DIM], w[..., MLA_NOPE_DIM:]
    w = jnp.concatenate([nope, pe[..., :half], z, pe[..., half:], z], axis=-1)
    return w.reshape(r, n_heads * MLA_QK_PAD)


def kernel(x, p, ln_mix_g, w_in, lam_q1, lam_k1, lam_q2, lam_k2, diff_norm_g, mla_qa_norm_g, w_uq, mla_kva_norm_g, w_ukv, w_out, ln_ffn_g, w_group_router, b_group_router, w_expert_router, b_expert_router, w_exp_gate, w_exp_up, w_exp_down, ln_ple_g, w_ple_gate, w_ple_proj, ln_final_g):
    batch, s, d = x.shape
    assert batch == 1
    depth = w_in.shape[0]
    diff_width = d // 2
    n_diff = diff_width // HEAD_V_DIM
    n_mla = (d // 2) // HEAD_V_DIM
    q_rank = mla_qa_norm_g.shape[1]
    kv_rank = mla_kva_norm_g.shape[1]
    v_start = 2 * diff_width
    cq_start = 3 * diff_width
    rope_start = cq_start + q_rank + kv_rank
    assert w_in.shape[2] == rope_start + MLA_ROPE_DIM
    t = _pick(s, ATTN_TILE, CHUNK)
    cos_t, sin_t = _rope_tables(s)

    h = x.reshape(s, d)
    for i in range(depth):
        w_in_i = w_in[i]
        w_main = jnp.concatenate([w_in_i[:, :v_start], w_in_i[:, cq_start:rope_start]], axis=1).astype(BF16)
        w_v_t = w_in_i[:, v_start:cq_start].T.astype(BF16)
        w_rope = _spread_rope_cols(w_in_i[:, rope_start:]).astype(BF16)
        hn = rmsnorm(h, ln_mix_g[i])
        proj = matmul(hn, w_main)
        vt_diff = matmul_nt_blocked(w_v_t, hn, t)
        k_pe = matmul(hn, w_rope, mode="rope_all", cos=cos_t, sin=sin_t)
        lam_rows = jnp.stack([lam_q1[i], lam_k1[i], lam_q2[i], lam_k2[i]]).astype(F32)
        y_diff = diff_attention(proj, vt_diff, lam_rows, diff_norm_g[i], i, n_diff)
        cqn = rmsnorm(proj, mla_qa_norm_g[i], col_start=v_start, width=q_rank)
        ckvn = rmsnorm(proj, mla_kva_norm_g[i], col_start=v_start + q_rank, width=kv_rank)
        q_mla = matmul(cqn, _pad_uq(w_uq[i], n_mla).astype(BF16), mode="rope_upper", cos=cos_t, sin=sin_t)
        w_ukv_i = w_ukv[i].reshape(kv_rank, n_mla, MLA_NOPE_DIM + HEAD_V_DIM)
        w_kn = w_ukv_i[:, :, :MLA_NOPE_DIM].reshape(kv_rank, n_mla * MLA_NOPE_DIM).astype(BF16)
        w_vm_t = w_ukv_i[:, :, MLA_NOPE_DIM:].reshape(kv_rank, n_mla * HEAD_V_DIM).T.astype(BF16)
        kn = matmul(ckvn, w_kn)
        vt_mla = matmul_nt_blocked(w_vm_t, ckvn, t)
        y_mla = mla_attention(q_mla, kn, k_pe, vt_mla, n_mla)
        y = jnp.concatenate([y_diff, y_mla], axis=1)
        h = matmul(y, w_out[i].astype(BF16), mode="residual", out_dtype=F32, res=h)
        h = hierarchical_moe(h, ln_ffn_g[i], w_group_router[i], b_group_router[i],
                             w_expert_router[i], b_expert_router[i],
                             w_exp_gate[i].astype(BF16), w_exp_up[i].astype(BF16),
                             w_exp_down[i].astype(BF16))
        hn = rmsnorm(h, ln_ple_g[i])
        h = matmul(hn, w_ple_gate[i].astype(BF16), mode="ple", out_dtype=F32, res=h,
                   p=p[i].reshape(s, -1), wp=w_ple_proj[i].astype(BF16))
    return rmsnorm(h, ln_final_g, out_dtype=x.dtype).reshape(batch, s, d)
```

```python
import functools
import math

import jax
import jax.numpy as jnp
from jax import lax
from jax.experimental import pallas as pl
from jax.experimental.pallas import tpu as pltpu

F32 = jnp.float32
BF16 = jnp.bfloat16

CHUNK = 64
RMS_EPS = 1e-6
NEG_INF = -1e30
DIFF_QK_DIM = 64
HEAD_V_DIM = 128
MLA_NOPE_DIM = 128
MLA_ROPE_DIM = 64
ROPE_BASE = 10000.0
N_GROUPS = 4
EXPERTS_PER_GROUP = 8
TOP_K = 2
LOG2_E = 1.4426950408889634

LANES = 128
MLA_QK_PAD = 256
V7X_VMEM_LIMIT = 56 * 1024 * 1024

NORM_ROWS = 256
MM_BM = 1024
MM_BN = 512
ATTN_TILE = 512
ATTN_Q_BLOCKS = 1
KEY_LOOP_UNROLLS = (8, 4, 2)
ONES_ROWS = 16
ROUTER_ROWS = 256
MOE_ROWS = 256
MOVE_ROWS = 256

_NT = (((1,), (1,)), ((), ()))


def _pick(n, pref, mult=LANES):
    if n <= pref:
        return n
    best = None
    for c in range(mult, pref + 1, mult):
        if n % c == 0:
            best = c
    assert best is not None, (n, pref, mult)
    return best


def _params(sem):
    return pltpu.CompilerParams(dimension_semantics=sem, vmem_limit_bytes=V7X_VMEM_LIMIT)


def _rmsnorm_kernel(x_ref, g_ref, o_ref):
    x = x_ref[...].astype(F32)
    ms = jnp.mean(x * x, axis=-1, keepdims=True)
    o_ref[...] = (x * lax.rsqrt(ms + RMS_EPS) * g_ref[...]).astype(o_ref.dtype)


def rmsnorm(x, g, *, col_start=0, width=None, out_dtype=BF16):
    s = x.shape[0]
    width = x.shape[1] if width is None else width
    assert col_start % width == 0
    cb = col_start // width
    bm = _pick(s, NORM_ROWS, 8)
    return pl.pallas_call(
        _rmsnorm_kernel,
        grid=(s // bm,),
        in_specs=[pl.BlockSpec((bm, width), lambda i: (i, cb)),
                  pl.BlockSpec((1, width), lambda i: (0, 0))],
        out_specs=pl.BlockSpec((bm, width), lambda i: (i, 0)),
        out_shape=jax.ShapeDtypeStruct((s, width), out_dtype),
        compiler_params=_params(("arbitrary",)),
        name="rmsnorm",
    )(x, g.reshape(1, width).astype(F32))


def _rope_lanes(x, cos, sin):
    return x * cos + pltpu.roll(x, LANES // 2, axis=1) * sin


def _mm_kernel(*refs, mode, bn):
    if mode == "plain":
        x_ref, w_ref, o_ref = refs
    elif mode in ("rope_upper", "rope_all"):
        x_ref, w_ref, cos_ref, sin_ref, o_ref = refs
    elif mode == "residual":
        x_ref, w_ref, res_ref, o_ref = refs
    elif mode == "ple":
        x_ref, w_ref, res_ref, p_ref, wp_ref, o_ref = refs
    acc = jnp.dot(x_ref[...], w_ref[...], preferred_element_type=F32)
    if mode == "plain":
        o_ref[...] = acc.astype(o_ref.dtype)
    elif mode == "rope_all":
        o_ref[...] = _rope_lanes(acc, cos_ref[...], sin_ref[...]).astype(o_ref.dtype)
    elif mode == "rope_upper":
        cos = cos_ref[...]
        sin = sin_ref[...]
        for g in range(bn // MLA_QK_PAD):
            lo = g * MLA_QK_PAD
            o_ref[:, lo:lo + LANES] = acc[:, lo:lo + LANES].astype(o_ref.dtype)
            o_ref[:, lo + LANES:lo + 2 * LANES] = _rope_lanes(
                acc[:, lo + LANES:lo + 2 * LANES], cos, sin).astype(o_ref.dtype)
    elif mode == "residual":
        o_ref[...] = res_ref[...] + acc
    elif mode == "ple":
        emb = jnp.dot(p_ref[...].astype(BF16), wp_ref[...], preferred_element_type=F32)
        o_ref[...] = res_ref[...] + jax.nn.sigmoid(acc) * emb


def matmul(x, w, *, mode="plain", out_dtype=BF16, cos=None, sin=None, res=None, p=None, wp=None):
    m, k = x.shape
    n = w.shape[1]
    bm = _pick(m, MM_BM, 8)
    bn = _pick(n, MM_BN, MLA_QK_PAD if mode == "rope_upper" else LANES)
    in_specs = [pl.BlockSpec((bm, k), lambda i, j: (i, 0)),
                pl.BlockSpec((k, bn), lambda i, j: (0, j))]
    args = [x, w]
    if mode in ("rope_upper", "rope_all"):
        in_specs += [pl.BlockSpec((bm, LANES), lambda i, j: (i, 0))] * 2
        args += [cos, sin]
    if mode in ("residual", "ple"):
        in_specs.append(pl.BlockSpec((bm, bn), lambda i, j: (i, j)))
        args.append(res)
    if mode == "ple":
        kp = p.shape[1]
        in_specs += [pl.BlockSpec((bm, kp), lambda i, j: (i, 0)),
                     pl.BlockSpec((kp, bn), lambda i, j: (0, j))]
        args += [p, wp]
    return pl.pallas_call(
        functools.partial(_mm_kernel, mode=mode, bn=bn),
        grid=(m // bm, n // bn),
        in_specs=in_specs,
        out_specs=pl.BlockSpec((bm, bn), lambda i, j: (i, j)),
        out_shape=jax.ShapeDtypeStruct((m, n), out_dtype),
        compiler_params=_params(("arbitrary", "arbitrary")),
        name="matmul_" + mode,
    )(*args)


def _mm_nt_kernel(a_ref, b_ref, o_ref):
    o_ref[...] = lax.dot_general(a_ref[...], b_ref[...], _NT,
                                 preferred_element_type=F32).astype(o_ref.dtype)


def matmul_nt_blocked(a, b, bn):
    m, k = a.shape
    n = b.shape[0]
    bm = _pick(m, MM_BN, 8)
    return pl.pallas_call(
        _mm_nt_kernel,
        grid=(m // bm, n // bn),
        in_specs=[pl.BlockSpec((bm, k), lambda i, j: (i, 0)),
                  pl.BlockSpec((bn, k), lambda i, j: (j, 0))],
        out_specs=pl.BlockSpec((None, bm, bn), lambda i, j: (j, i, 0)),
        out_shape=jax.ShapeDtypeStruct((n // bn, m, bn), BF16),
        compiler_params=_params(("arbitrary", "arbitrary")),
        name="matmul_nt",
    )(a, b)


def _softmax_block(s_t, off, vt, m_ref, acc_ref, c):
    m_prev = m_ref[c]
    m_new = jnp.maximum(m_prev, jnp.max(s_t, axis=0, keepdims=True) - off)
    alpha = jnp.exp2(m_prev - m_new)
    p_t = jnp.exp2(s_t - (m_new + off))
    acc_ref[c] = alpha * acc_ref[c] + jnp.dot(vt, p_t.astype(BF16), preferred_element_type=F32)
    m_ref[c] = m_new


def _values_with_ones(vt_ref, kb, t):
    ones_rows = (lax.broadcasted_iota(jnp.int32, (ONES_ROWS, t), 0) == 0).astype(BF16)
    return jnp.concatenate([vt_ref[kb], ones_rows], axis=0)


def _pipelined_key_loop(n_full, n_diag, scores, consume, s_a, s_b):
    assert n_diag in (1, 2)
    scores(0, s_a)

    def two_blocks(kb):
        scores(kb + 1, s_b)
        consume(s_a, kb, None)
        scores(kb + 2, s_a)
        consume(s_b, kb + 1, None)

    done = 0
    for unroll in KEY_LOOP_UNROLLS:
        count = (n_full - done) // unroll

        def unrolled(j, carry, unroll=unroll, done=done):
            for u in range(0, unroll, 2):
                two_blocks(done + unroll * j + u)
            return carry

        lax.fori_loop(0, count, unrolled, 0)
        done = done + unroll * count

    if n_diag == 2:
        scores(n_full + 1, s_b)
        consume(s_a, n_full, 0)
        consume(s_b, n_full + 1, 1)
        return

    @pl.when(n_full % 2 == 1)
    def _():
        scores(n_full, s_b)
        consume(s_a, n_full - 1, None)
        consume(s_b, n_full, 0)

    @pl.when(n_full % 2 == 0)
    def _():
        consume(s_a, n_full, 0)


def _init_stats(acc_ref, m_ref):
    acc_ref[...] = jnp.zeros_like(acc_ref)
    m_ref[...] = jnp.full_like(m_ref, NEG_INF)


def _normalised(acc_ref, c):
    acc = acc_ref[c]
    return acc[:HEAD_V_DIM] / acc[HEAD_V_DIM:HEAD_V_DIM + 1]


def _split3(x):
    hi = x.astype(BF16).astype(F32)
    r1 = x - hi
    mid = r1.astype(BF16).astype(F32)
    lo = (r1 - mid).astype(BF16).astype(F32)
    return hi, mid, lo


def _by_lane(lane, columns):
    out = jnp.zeros(lane.shape, F32)
    for i, col in enumerate(columns):
        out = jnp.where(lane == i, col, out)
    return out.astype(BF16)


def _diagonal_terms(diag_ref, slope):
    n_diag, t, tq = diag_ref.shape
    key = lax.broadcasted_iota(jnp.int32, (t, tq), 0)
    qry = lax.broadcasted_iota(jnp.int32, (t, tq), 1)
    for d in range(n_diag):
        allowed = ((key + d * t) // CHUNK) <= (qry // CHUNK)
        if slope is None:
            term = jnp.zeros((t, tq), F32)
        else:
            term = 2.0 * slope * jnp.minimum((qry - key - d * t).astype(F32), 0.0)
        diag_ref[d] = jnp.where(allowed, term, NEG_INF)


def _diff_attn_kernel(slopes_ref, q_ref, k_ref, vt_ref, lam_ref, g_ref, o_ref,
                      s_a, s_b, diag_ref, qbias_ref, kbias_ref, acc_ref, m_ref, *, t, lam_init):
    h = pl.program_id(0)
    qb = pl.program_id(1)
    tq = q_ref.shape[0]
    n_diag = tq // t
    slope = slopes_ref[h] * LOG2_E

    @pl.when(qb == 0)
    def _():
        lane_q = lax.broadcasted_iota(jnp.int32, (tq, LANES), 1)
        pos_q = lax.broadcasted_iota(jnp.int32, (tq, LANES), 0)
        slope_v = jnp.full((tq, LANES), slope, F32)
        s_hi, s_mid, s_lo = _split3(slope_v)
        a_hi, a_mid, a_lo = _split3(slope_v * pos_q.astype(F32))
        qbias_ref[...] = _by_lane(lane_q, [s_hi, s_hi, s_mid, s_mid, s_lo, s_lo, -a_hi, -a_mid, -a_lo])
        lane_k = lax.broadcasted_iota(jnp.int32, (t, LANES), 1)
        pos_k = lax.broadcasted_iota(jnp.int32, (t, LANES), 0)
        one = jnp.ones((t, LANES), F32)
        j_even = (pos_k & ~1).astype(F32)
        j_odd = (pos_k & 1).astype(F32)
        kbias_ref[...] = _by_lane(lane_k, [j_even, j_odd, j_even, j_odd, j_even, j_odd, one, one, one])
        _diagonal_terms(diag_ref, slope)

    q = (q_ref[...].astype(F32) * (DIFF_QK_DIM ** -0.5 * LOG2_E)).astype(BF16)
    lane = lax.broadcasted_iota(jnp.int32, q.shape, 1)
    zero = jnp.zeros_like(q)
    q_maps = (jnp.where(lane < DIFF_QK_DIM, q, zero), jnp.where(lane >= DIFF_QK_DIM, q, zero))
    q_aug = tuple(jnp.concatenate([qm, qbias_ref[...]], axis=1) for qm in q_maps)
    _init_stats(acc_ref, m_ref)

    def scores(kb, s_ref):
        k = jnp.concatenate([k_ref[pl.ds(pl.multiple_of(kb * t, t), t), :], kbias_ref[...]], axis=1)
        for c in range(2):
            s_ref[c] = lax.dot_general(k, q_aug[c], _NT, preferred_element_type=F32)

    def consume(s_ref, kb, diag):
        vt = _values_with_ones(vt_ref, kb, t)
        off = slope * ((n_diag * qb - kb) * t).astype(F32)
        for c in range(2):
            s_t = s_ref[c] if diag is None else s_ref[c] + diag_ref[diag]
            _softmax_block(s_t, off, vt, m_ref, acc_ref, c)

    _pipelined_key_loop(n_diag * qb, n_diag, scores, consume, s_a, s_b)

    lam_rows = lam_ref[...]
    lam = (jnp.exp(jnp.sum(lam_rows[0:1] * lam_rows[1:2], axis=1, keepdims=True))
           - jnp.exp(jnp.sum(lam_rows[2:3] * lam_rows[3:4], axis=1, keepdims=True)) + lam_init)
    d_t = _normalised(acc_ref, 0) - lam * _normalised(acc_ref, 1)
    ms = jnp.mean(d_t * d_t, axis=0, keepdims=True)
    y_t = d_t * lax.rsqrt(ms + RMS_EPS) * g_ref[...] * (1.0 - lam_init)
    o_ref[...] = y_t.T.astype(o_ref.dtype)


def diff_attention(proj, vt, lam_rows, norm_g, layer, n_heads):
    s = proj.shape[0]
    t = vt.shape[2]
    tq = ATTN_Q_BLOCKS * t
    lam_init = 0.8 - 0.6 * math.exp(-0.3 * layer)
    slopes = 2.0 ** (-8.0 * jnp.arange(1, n_heads + 1, dtype=F32) / n_heads)
    grid_spec = pltpu.PrefetchScalarGridSpec(
        num_scalar_prefetch=1,
        grid=(n_heads, s // tq),
        in_specs=[pl.BlockSpec((tq, LANES), lambda h, i, sl: (i, h)),
                  pl.BlockSpec((s, LANES), lambda h, i, sl: (0, n_heads + h)),
                  pl.BlockSpec((s // t, HEAD_V_DIM, t), lambda h, i, sl: (0, h, 0)),
                  pl.BlockSpec((4, DIFF_QK_DIM), lambda h, i, sl: (0, 0)),
                  pl.BlockSpec((HEAD_V_DIM, 1), lambda h, i, sl: (0, 0))],
        out_specs=pl.BlockSpec((tq, HEAD_V_DIM), lambda h, i, sl: (i, h)),
        scratch_shapes=[pltpu.VMEM((2, t, tq), F32),
                        pltpu.VMEM((2, t, tq), F32),
                        pltpu.VMEM((ATTN_Q_BLOCKS, t, tq), F32),
                        pltpu.VMEM((tq, LANES), BF16),
                        pltpu.VMEM((t, LANES), BF16),
                        pltpu.VMEM((2, HEAD_V_DIM + ONES_ROWS, tq), F32),
                        pltpu.VMEM((2, 1, tq), F32)])
    return pl.pallas_call(
        functools.partial(_diff_attn_kernel, t=t, lam_init=lam_init),
        grid_spec=grid_spec,
        out_shape=jax.ShapeDtypeStruct((s, n_heads * HEAD_V_DIM), BF16),
        compiler_params=_params(("arbitrary", "arbitrary")),
        name="diff_attention",
    )(slopes, proj, proj, vt, lam_rows, norm_g.reshape(HEAD_V_DIM, 1).astype(F32))


def _mla_attn_kernel(q_ref, kn_ref, kpe_ref, vt_ref, o_ref, s_a, s_b, diag_ref, acc_ref, m_ref, *, t):
    qb = pl.program_id(1)
    n_diag = q_ref.shape[0] // t

    @pl.when(qb == 0)
    def _():
        _diagonal_terms(diag_ref, None)

    scale = (MLA_NOPE_DIM + MLA_ROPE_DIM) ** -0.5 * LOG2_E
    q = (q_ref[...].astype(F32) * scale).astype(BF16)
    _init_stats(acc_ref, m_ref)

    def scores(kb, s_ref):
        k0 = pl.multiple_of(kb * t, t)
        k = jnp.concatenate([kn_ref[pl.ds(k0, t), :], kpe_ref[pl.ds(k0, t), :]], axis=1)
        s_ref[0] = lax.dot_general(k, q, _NT, preferred_element_type=F32)

    def consume(s_ref, kb, diag):
        s_t = s_ref[0] if diag is None else s_ref[0] + diag_ref[diag]
        _softmax_block(s_t, 0.0, _values_with_ones(vt_ref, kb, t), m_ref, acc_ref, 0)

    _pipelined_key_loop(n_diag * qb, n_diag, scores, consume, s_a, s_b)
    o_ref[...] = _normalised(acc_ref, 0).T.astype(o_ref.dtype)


def mla_attention(q, kn, k_pe, vt, n_heads):
    s = q.shape[0]
    t = vt.shape[2]
    tq = ATTN_Q_BLOCKS * t
    return pl.pallas_call(
        functools.partial(_mla_attn_kernel, t=t),
        grid=(n_heads, s // tq),
        in_specs=[pl.BlockSpec((tq, MLA_QK_PAD), lambda h, i: (i, h)),
                  pl.BlockSpec((s, LANES), lambda h, i: (0, h)),
                  pl.BlockSpec((s, LANES), lambda h, i: (0, 0)),
                  pl.BlockSpec((s // t, HEAD_V_DIM, t), lambda h, i: (0, h, 0))],
        out_specs=pl.BlockSpec((tq, HEAD_V_DIM), lambda h, i: (i, h)),
        out_shape=jax.ShapeDtypeStruct((s, n_heads * HEAD_V_DIM), BF16),
        scratch_shapes=[pltpu.VMEM((1, t, tq), F32),
                        pltpu.VMEM((1, t, tq), F32),
                        pltpu.VMEM((ATTN_Q_BLOCKS, t, tq), F32),
                        pltpu.VMEM((1, HEAD_V_DIM + ONES_ROWS, tq), F32),
                        pltpu.VMEM((1, 1, tq), F32)],
        compiler_params=_params(("arbitrary", "arbitrary")),
        name="mla_attention",
    )(q, kn, k_pe, vt)


def _router_kernel(h_ref, g_ref, w_ref, b_ref, xn_ref, meta_i_ref, meta_f_ref, cnt_ref, carry_ref,
                   *, n_exp):
    tb = h_ref.shape[0]

    @pl.when(pl.program_id(0) == 0)
    def _():
        carry_ref[...] = jnp.zeros_like(carry_ref)

    x = h_ref[...]
    ms = jnp.mean(x * x, axis=-1, keepdims=True)
    xn = x * lax.rsqrt(ms + RMS_EPS) * g_ref[...]
    xn_ref[...] = xn.astype(xn_ref.dtype)
    logits = jnp.dot(xn, w_ref[...], preferred_element_type=F32,
                     precision=lax.Precision.HIGHEST) + b_ref[...]
    lane = lax.broadcasted_iota(jnp.int32, logits.shape, 1)
    lane_f = lane.astype(F32)
    big = jnp.asarray(4.0 * LANES, F32)

    def first_lane(cond):
        return jnp.min(jnp.where(cond, lane_f, big), axis=1, keepdims=True)

    is_group = (lane >= n_exp) & (lane < n_exp + N_GROUPS)
    g_logit = jnp.where(is_group, logits, NEG_INF)
    g_max = jnp.max(g_logit, axis=1, keepdims=True)
    g_sum = jnp.sum(jnp.where(is_group, jnp.exp(g_logit - g_max), 0.0), axis=1, keepdims=True)
    g_val = 1.0 / g_sum
    g_idx = first_lane(is_group & (g_logit == g_max)) - n_exp
    in_group = (lane < n_exp) & ((lane // EXPERTS_PER_GROUP).astype(F32) == g_idx)
    e_logit = jnp.where(in_group, logits, NEG_INF)
    top1 = jnp.max(e_logit, axis=1, keepdims=True)
    idx1 = first_lane(in_group & (e_logit == top1))
    e_logit2 = jnp.where(lane_f == idx1, NEG_INF, e_logit)
    top2 = jnp.max(e_logit2, axis=1, keepdims=True)
    idx2 = first_lane(in_group & (lane_f != idx1) & (e_logit2 == top2))
    ratio = jnp.exp(top2 - top1)
    gate1 = g_val / (1.0 + ratio)
    gate2 = g_val * ratio / (1.0 + ratio)

    sel1 = lane_f == idx1
    sel2 = lane_f == idx2
    onehot = jnp.where(sel1 | sel2, 1.0, 0.0)
    r = lax.broadcasted_iota(jnp.int32, (tb, tb), 0)
    c = lax.broadcasted_iota(jnp.int32, (tb, tb), 1)
    before = jnp.where(c < r, 1.0, 0.0).astype(BF16)
    pos = jnp.dot(before, onehot.astype(BF16), preferred_element_type=F32) + carry_ref[...]
    rank1 = jnp.sum(jnp.where(sel1, pos, 0.0), axis=1, keepdims=True)
    rank2 = jnp.sum(jnp.where(sel2, pos, 0.0), axis=1, keepdims=True)
    carry_ref[...] = carry_ref[...] + jnp.sum(onehot, axis=0, keepdims=True)
    cnt_ref[...] = carry_ref[...]

    def lanes4(a, b, c4, d):
        return jnp.where(lane == 0, a, jnp.where(lane == 1, b, jnp.where(lane == 2, c4, d)))

    meta_i_ref[...] = lanes4(idx1, idx2, rank1, rank2).astype(jnp.int32)
    meta_f_ref[...] = lanes4(gate1, gate2, 0.0, 0.0)


def moe_router(h, gain, w_group, b_group, w_expert, b_expert):
    s, d = h.shape
    n_exp = w_expert.shape[1]
    assert n_exp == N_GROUPS * EXPERTS_PER_GROUP and n_exp + N_GROUPS <= LANES
    pad = LANES - n_exp - N_GROUPS
    w_cat = jnp.concatenate([w_expert, w_group, jnp.zeros((d, pad), F32)], axis=1).astype(F32)
    b_cat = jnp.concatenate([b_expert, b_group, jnp.zeros((pad,), F32)]).reshape(1, LANES).astype(F32)
    tb = _pick(s, ROUTER_ROWS, 8)
    return pl.pallas_call(
        functools.partial(_router_kernel, n_exp=n_exp),
        grid=(s // tb,),
        in_specs=[pl.BlockSpec((tb, d), lambda i: (i, 0)),
                  pl.BlockSpec((1, d), lambda i: (0, 0)),
                  pl.BlockSpec((d, LANES), lambda i: (0, 0)),
                  pl.BlockSpec((1, LANES), lambda i: (0, 0))],
        out_specs=[pl.BlockSpec((tb, d), lambda i: (i, 0)),
                   pl.BlockSpec((tb, LANES), lambda i: (i, 0)),
                   pl.BlockSpec((tb, LANES), lambda i: (i, 0)),
                   pl.BlockSpec((1, LANES), lambda i: (0, 0))],
        out_shape=[jax.ShapeDtypeStruct((s, d), F32),
                   jax.ShapeDtypeStruct((s, LANES), jnp.int32),
                   jax.ShapeDtypeStruct((s, LANES), F32),
                   jax.ShapeDtypeStruct((1, LANES), F32)],
        scratch_shapes=[pltpu.VMEM((1, LANES), F32)],
        compiler_params=_params(("arbitrary",)),
        name="moe_router",
    )(h, gain.reshape(1, d).astype(F32), w_cat, b_cat)


def _scatter_rows_kernel(d0_ref, d1_ref, x_ref, xs_init_ref, xs_ref, sems):
    del xs_init_ref
    tb = x_ref.shape[0]
    base = pl.program_id(0) * tb

    def copies(t):
        row = x_ref.at[pl.ds(t, 1), :]
        return (pltpu.make_async_copy(row, xs_ref.at[pl.ds(d0_ref[base + t], 1), :], sems.at[0]),
                pltpu.make_async_copy(row, xs_ref.at[pl.ds(d1_ref[base + t], 1), :], sems.at[1]))

    def start(t, carry):
        for cp in copies(t):
            cp.start()
        return carry

    def wait(t, carry):
        for cp in copies(t):
            cp.wait()
        return carry

    lax.fori_loop(0, tb, start, 0)
    lax.fori_loop(0, tb, wait, 0)


def scatter_rows(x, dest0, dest1, n_rows):
    n, d = x.shape
    tb = _pick(n, MOVE_ROWS, 8)
    grid_spec = pltpu.PrefetchScalarGridSpec(
        num_scalar_prefetch=2,
        grid=(n // tb,),
        in_specs=[pl.BlockSpec((tb, d), lambda i, a, b: (i, 0)),
                  pl.BlockSpec(memory_space=pl.ANY)],
        out_specs=pl.BlockSpec(memory_space=pl.ANY),
        scratch_shapes=[pltpu.SemaphoreType.DMA((2,))])
    return pl.pallas_call(
        _scatter_rows_kernel,
        grid_spec=grid_spec,
        out_shape=jax.ShapeDtypeStruct((n_rows, d), x.dtype),
        input_output_aliases={3: 0},
        compiler_params=_params(("arbitrary",)),
        name="moe_scatter_rows",
    )(dest0, dest1, x, jnp.zeros((n_rows, d), x.dtype))


def _expert_up_kernel(be_ref, nv_ref, x_ref, wg_ref, wu_ref, o_ref):
    b = pl.program_id(0)

    @pl.when(b < nv_ref[0])
    def _():
        x = x_ref[...].astype(BF16)
        gate = jnp.dot(x, wg_ref[...], preferred_element_type=F32)
        up = jnp.dot(x, wu_ref[...], preferred_element_type=F32)
        o_ref[...] = (jax.nn.silu(gate) * up).astype(o_ref.dtype)

    @pl.when(b >= nv_ref[0])
    def _():
        o_ref[...] = jnp.zeros_like(o_ref)


def _expert_down_kernel(be_ref, nv_ref, x_ref, wd_ref, o_ref):
    b = pl.program_id(0)

    @pl.when(b < nv_ref[0])
    def _():
        o_ref[...] = jnp.dot(x_ref[...], wd_ref[...], preferred_element_type=F32).astype(o_ref.dtype)

    @pl.when(b >= nv_ref[0])
    def _():
        o_ref[...] = jnp.zeros_like(o_ref)


def expert_ffn(xs, blk_e, n_valid, w_gate, w_up, w_down, layer):
    n_rows, d = xs.shape
    de = w_gate.shape[3]
    rb = MOE_ROWS
    n_blk = n_rows // rb
    up_spec = pltpu.PrefetchScalarGridSpec(
        num_scalar_prefetch=2,
        grid=(n_blk,),
        in_specs=[pl.BlockSpec((rb, d), lambda b, be, nv: (b, 0)),
                  pl.BlockSpec((None, None, d, de), lambda b, be, nv: (layer, be[b], 0, 0)),
                  pl.BlockSpec((None, None, d, de), lambda b, be, nv: (layer, be[b], 0, 0))],
        out_specs=pl.BlockSpec((rb, de), lambda b, be, nv: (b, 0)))
    hid = pl.pallas_call(
        _expert_up_kernel,
        grid_spec=up_spec,
        out_shape=jax.ShapeDtypeStruct((n_rows, de), BF16),
        compiler_params=_params(("arbitrary",)),
        name="moe_expert_up",
    )(blk_e, n_valid, xs, w_gate, w_up)
    down_spec = pltpu.PrefetchScalarGridSpec(
        num_scalar_prefetch=2,
        grid=(n_blk,),
        in_specs=[pl.BlockSpec((rb, de), lambda b, be, nv: (b, 0)),
                  pl.BlockSpec((None, None, de, d), lambda b, be, nv: (layer, be[b], 0, 0))],
        out_specs=pl.BlockSpec((rb, d), lambda b, be, nv: (b, 0)))
    return pl.pallas_call(
        _expert_down_kernel,
        grid_spec=down_spec,
        out_shape=jax.ShapeDtypeStruct((n_rows, d), F32),
        compiler_params=_params(("arbitrary",)),
        name="moe_expert_down",
    )(blk_e, n_valid, hid, w_down)


def _combine_kernel(d0_ref, d1_ref, h_ref, gate_ref, y_ref, o_ref, buf_ref, sems):
    tb = h_ref.shape[0]
    base = pl.program_id(0) * tb

    def copies(t):
        return (pltpu.make_async_copy(y_ref.at[pl.ds(d0_ref[base + t], 1), :],
                                      buf_ref.at[0, pl.ds(t, 1), :], sems.at[0]),
                pltpu.make_async_copy(y_ref.at[pl.ds(d1_ref[base + t], 1), :],
                                      buf_ref.at[1, pl.ds(t, 1), :], sems.at[1]))

    def start(t, carry):
        for cp in copies(t):
            cp.start()
        return carry

    def wait(t, carry):
        for cp in copies(t):
            cp.wait()
        return carry

    lax.fori_loop(0, tb, start, 0)
    lax.fori_loop(0, tb, wait, 0)
    gates = gate_ref[...]
    o_ref[...] = h_ref[...] + gates[:, 0:1] * buf_ref[0] + gates[:, 1:2] * buf_ref[1]


def combine_rows(h, y, dest0, dest1, gates):
    n, d = h.shape
    tb = _pick(n, MOVE_ROWS, 8)
    grid_spec = pltpu.PrefetchScalarGridSpec(
        num_scalar_prefetch=2,
        grid=(n // tb,),
        in_specs=[pl.BlockSpec((tb, d), lambda i, a, b: (i, 0)),
                  pl.BlockSpec((tb, LANES), lambda i, a, b: (i, 0)),
                  pl.BlockSpec(memory_space=pl.ANY)],
        out_specs=pl.BlockSpec((tb, d), lambda i, a, b: (i, 0)),
        scratch_shapes=[pltpu.VMEM((2, tb, d), y.dtype),
                        pltpu.SemaphoreType.DMA((2,))])
    return pl.pallas_call(
        _combine_kernel,
        grid_spec=grid_spec,
        out_shape=jax.ShapeDtypeStruct(h.shape, F32),
        compiler_params=_params(("arbitrary",)),
        name="moe_combine_rows",
    )(dest0, dest1, h, gates, y)


def hierarchical_moe(h, gain, w_group, b_group, w_expert, b_expert, w_gate, w_up, w_down, layer):
    s, d = h.shape
    n_exp = w_expert.shape[1]
    rb = MOE_ROWS
    xn, meta_i, meta_f, cnt = moe_router(h, gain, w_group, b_group, w_expert, b_expert)
    counts = cnt[0, :n_exp].astype(jnp.int32)
    padded = (counts + rb - 1) // rb * rb
    pend = jnp.cumsum(padded)
    pstart = pend - padded
    eid = meta_i[:, :TOP_K]
    dest = pstart[eid] + meta_i[:, TOP_K:2 * TOP_K]
    n_rows = (s * TOP_K + n_exp * (rb - 1) + rb - 1) // rb * rb
    n_blk = n_rows // rb
    blk_start = jnp.arange(n_blk, dtype=jnp.int32) * rb
    blk_e = jnp.minimum(jnp.sum((pend[None, :] <= blk_start[:, None]).astype(jnp.int32), axis=1),
                        n_exp - 1).astype(jnp.int32)
    n_valid = (pend[-1:] // rb).astype(jnp.int32)
    dest0, dest1 = dest[:, 0], dest[:, 1]
    xs = scatter_rows(xn, dest0, dest1, n_rows)
    y = expert_ffn(xs, blk_e, n_valid, w_gate, w_up, w_down, layer)
    return combine_rows(h, y, dest0, dest1, meta_f)


def _rope_tables(s):
    half = MLA_ROPE_DIM // 2
    inv_freq = 1.0 / (ROPE_BASE ** (jnp.arange(0, MLA_ROPE_DIM, 2, dtype=F32) / MLA_ROPE_DIM))
    ang = jnp.arange(s, dtype=F32)[:, None] * inv_freq[None, :]
    cos, sin = jnp.cos(ang), jnp.sin(ang)
    z = jnp.zeros((s, half), F32)
    return (jnp.concatenate([cos, z, cos, z], axis=1), jnp.concatenate([-sin, z, sin, z], axis=1))


def _spread_rope_cols(w):
    half = MLA_ROPE_DIM // 2
    z = jnp.zeros((w.shape[0], half), w.dtype)
    return jnp.concatenate([w[:, :half], z, w[:, half:], z], axis=1)


def _pad_uq(w_uq, n_heads):
    r = w_uq.shape[0]
    w = w_uq.reshape(r, n_heads, MLA_NOPE_DIM + MLA_ROPE_DIM)
    half = MLA_ROPE_DIM // 2
    z = jnp.zeros((r, n_heads, half), w.dtype)
    nope, pe = w[..., :MLA_NOPE_DIM], w[..., MLA_NOPE_DIM:]
    w = jnp.concatenate([nope, pe[..., :half], z, pe[..., half:], z], axis=-1)
    return w.reshape(r, n_heads * MLA_QK_PAD)


def kernel(x, p, ln_mix_g, w_in, lam_q1, lam_k1, lam_q2, lam_k2, diff_norm_g, mla_qa_norm_g, w_uq, mla_kva_norm_g, w_ukv, w_out, ln_ffn_g, w_group_router, b_group_router, w_expert_router, b_expert_router, w_exp_gate, w_exp_up, w_exp_down, ln_ple_g, w_ple_gate, w_ple_proj, ln_final_g):
    batch, s, d = x.shape
    assert batch == 1
    depth = w_in.shape[0]
    diff_width = d // 2
    n_diff = diff_width // HEAD_V_DIM
    n_mla = (d // 2) // HEAD_V_DIM
    q_rank = mla_qa_norm_g.shape[1]
    kv_rank = mla_kva_norm_g.shape[1]
    v_start = 2 * diff_width
    cq_start = 3 * diff_width
    rope_start = cq_start + q_rank + kv_rank
    assert w_in.shape[2] == rope_start + MLA_ROPE_DIM
    t = _pick(s, ATTN_TILE, CHUNK)
    cos_t, sin_t = _rope_tables(s)
    w_gate_all = w_exp_gate.astype(BF16)
    w_up_all = w_exp_up.astype(BF16)
    w_down_all = w_exp_down.astype(BF16)

    h = x.reshape(s, d)
    for i in range(depth):
        w_in_i = w_in[i]
        w_main = jnp.concatenate([w_in_i[:, :v_start], w_in_i[:, cq_start:rope_start]], axis=1).astype(BF16)
        w_v_t = w_in_i[:, v_start:cq_start].T.astype(BF16)
        w_rope = _spread_rope_cols(w_in_i[:, rope_start:]).astype(BF16)
        hn = rmsnorm(h, ln_mix_g[i])
        proj = matmul(hn, w_main)
        vt_diff = matmul_nt_blocked(w_v_t, hn, t)
        k_pe = matmul(hn, w_rope, mode="rope_all", cos=cos_t, sin=sin_t)
        lam_rows = jnp.stack([lam_q1[i], lam_k1[i], lam_q2[i], lam_k2[i]]).astype(F32)
        y_diff = diff_attention(proj, vt_diff, lam_rows, diff_norm_g[i], i, n_diff)
        cqn = rmsnorm(proj, mla_qa_norm_g[i], col_start=v_start, width=q_rank)
        ckvn = rmsnorm(proj, mla_kva_norm_g[i], col_start=v_start + q_rank, width=kv_rank)
        q_mla = matmul(cqn, _pad_uq(w_uq[i], n_mla).astype(BF16), mode="rope_upper", cos=cos_t, sin=sin_t)
        w_ukv_i = w_ukv[i].reshape(kv_rank, n_mla, MLA_NOPE_DIM + HEAD_V_DIM)
        w_kn = w_ukv_i[:, :, :MLA_NOPE_DIM].reshape(kv_rank, n_mla * MLA_NOPE_DIM).astype(BF16)
        w_vm_t = w_ukv_i[:, :, MLA_NOPE_DIM:].reshape(kv_rank, n_mla * HEAD_V_DIM).T.astype(BF16)
        kn = matmul(ckvn, w_kn)
        vt_mla = matmul_nt_blocked(w_vm_t, ckvn, t)
        y_mla = mla_attention(q_mla, kn, k_pe, vt_mla, n_mla)
        y = jnp.concatenate([y_diff, y_mla], axis=1)
        h = matmul(y, w_out[i].astype(BF16), mode="residual", out_dtype=F32, res=h)
        h = hierarchical_moe(h, ln_ffn_g[i], w_group_router[i], b_group_router[i],
                             w_expert_router[i], b_expert_router[i],
                             w_gate_all, w_up_all, w_down_all, i)
        hn = rmsnorm(h, ln_ple_g[i])
        h = matmul(hn, w_ple_gate[i].astype(BF16), mode="ple", out_dtype=F32, res=h,
                   p=p[i].reshape(s, -1), wp=w_ple_proj[i].astype(BF16))
    return rmsnorm(h, ln_final_g, out_dtype=x.dtype).reshape(batch, s, d)
```

```python
import functools
import math

import jax
import jax.numpy as jnp
from jax import lax
from jax.experimental import pallas as pl
from jax.experimental.pallas import tpu as pltpu

F32 = jnp.float32
BF16 = jnp.bfloat16

CHUNK = 64
RMS_EPS = 1e-6
NEG_INF = -1e30
DIFF_QK_DIM = 64
HEAD_V_DIM = 128
MLA_NOPE_DIM = 128
MLA_ROPE_DIM = 64
ROPE_BASE = 10000.0
N_GROUPS = 4
EXPERTS_PER_GROUP = 8
TOP_K = 2
LOG2_E = 1.4426950408889634

LANES = 128
MLA_QK_PAD = 256
V7X_VMEM_LIMIT = 56 * 1024 * 1024

NORM_ROWS = 256
MM_BM = 1024
MM_BN = 512
ATTN_TILE = 512
ATTN_Q_BLOCKS = 1
KEY_LOOP_UNROLLS = (8, 4, 2)
ONES_ROWS = 16
ROUTER_ROWS = 256
MOE_ROWS = 256
MOVE_ROWS = 256

_NT = (((1,), (1,)), ((), ()))


def _pick(n, pref, mult=LANES):
    if n <= pref:
        return n
    best = None
    for c in range(mult, pref + 1, mult):
        if n % c == 0:
            best = c
    assert best is not None, (n, pref, mult)
    return best


def _params(sem):
    return pltpu.CompilerParams(dimension_semantics=sem, vmem_limit_bytes=V7X_VMEM_LIMIT)


def _rmsnorm_kernel(x_ref, g_ref, o_ref):
    x = x_ref[...].astype(F32)
    ms = jnp.mean(x * x, axis=-1, keepdims=True)
    o_ref[...] = (x * lax.rsqrt(ms + RMS_EPS) * g_ref[...]).astype(o_ref.dtype)


def rmsnorm(x, g, *, col_start=0, width=None, out_dtype=BF16):
    s = x.shape[0]
    width = x.shape[1] if width is None else width
    assert col_start % width == 0
    cb = col_start // width
    bm = _pick(s, NORM_ROWS, 8)
    return pl.pallas_call(
        _rmsnorm_kernel,
        grid=(s // bm,),
        in_specs=[pl.BlockSpec((bm, width), lambda i: (i, cb)),
                  pl.BlockSpec((1, width), lambda i: (0, 0))],
        out_specs=pl.BlockSpec((bm, width), lambda i: (i, 0)),
        out_shape=jax.ShapeDtypeStruct((s, width), out_dtype),
        compiler_params=_params(("arbitrary",)),
        name="rmsnorm",
    )(x, g.reshape(1, width).astype(F32))


def _rope_lanes(x, cos, sin):
    return x * cos + pltpu.roll(x, LANES // 2, axis=1) * sin


def _mm_kernel(*refs, mode, bn):
    if mode == "plain":
        x_ref, w_ref, o_ref = refs
    elif mode in ("rope_upper", "rope_all"):
        x_ref, w_ref, cos_ref, sin_ref, o_ref = refs
    elif mode == "residual":
        x_ref, w_ref, res_ref, o_ref = refs
    elif mode == "residual2":
        x_ref, w_ref, x2_ref, w2_ref, res_ref, o_ref = refs
    elif mode == "ple":
        x_ref, w_ref, res_ref, p_ref, wp_ref, o_ref = refs
    acc = jnp.dot(x_ref[...], w_ref[...], preferred_element_type=F32)
    if mode == "residual2":
        acc = acc + jnp.dot(x2_ref[...], w2_ref[...], preferred_element_type=F32)
    if mode == "plain":
        o_ref[...] = acc.astype(o_ref.dtype)
    elif mode == "rope_all":
        o_ref[...] = _rope_lanes(acc, cos_ref[...], sin_ref[...]).astype(o_ref.dtype)
    elif mode == "rope_upper":
        cos = cos_ref[...]
        sin = sin_ref[...]
        for g in range(bn // MLA_QK_PAD):
            lo = g * MLA_QK_PAD
            o_ref[:, lo:lo + LANES] = acc[:, lo:lo + LANES].astype(o_ref.dtype)
            o_ref[:, lo + LANES:lo + 2 * LANES] = _rope_lanes(
                acc[:, lo + LANES:lo + 2 * LANES], cos, sin).astype(o_ref.dtype)
    elif mode in ("residual", "residual2"):
        o_ref[...] = res_ref[...] + acc
    elif mode == "ple":
        emb = jnp.dot(p_ref[...].astype(BF16), wp_ref[...], preferred_element_type=F32)
        o_ref[...] = res_ref[...] + jax.nn.sigmoid(acc) * emb


def matmul(x, w, *, mode="plain", out_dtype=BF16, cos=None, sin=None, res=None, p=None, wp=None, x2=None):
    m, k = x.shape
    n = w.shape[1]
    bm = _pick(m, MM_BM, 8)
    bn = _pick(n, MM_BN, MLA_QK_PAD if mode == "rope_upper" else LANES)
    in_specs = [pl.BlockSpec((bm, k), lambda i, j: (i, 0)),
                pl.BlockSpec((k, bn), lambda i, j: (0, j))]
    args = [x, w]
    if mode == "residual2":
        assert x2.shape == x.shape and w.shape[0] == 2 * k
        in_specs += [pl.BlockSpec((bm, k), lambda i, j: (i, 0)),
                     pl.BlockSpec((k, bn), lambda i, j: (1, j))]
        args += [x2, w]
    if mode in ("rope_upper", "rope_all"):
        in_specs += [pl.BlockSpec((bm, LANES), lambda i, j: (i, 0))] * 2
        args += [cos, sin]
    if mode in ("residual", "residual2", "ple"):
        in_specs.append(pl.BlockSpec((bm, bn), lambda i, j: (i, j)))
        args.append(res)
    if mode == "ple":
        kp = p.shape[1]
        in_specs += [pl.BlockSpec((bm, kp), lambda i, j: (i, 0)),
                     pl.BlockSpec((kp, bn), lambda i, j: (0, j))]
        args += [p, wp]
    return pl.pallas_call(
        functools.partial(_mm_kernel, mode=mode, bn=bn),
        grid=(m // bm, n // bn),
        in_specs=in_specs,
        out_specs=pl.BlockSpec((bm, bn), lambda i, j: (i, j)),
        out_shape=jax.ShapeDtypeStruct((m, n), out_dtype),
        compiler_params=_params(("arbitrary", "arbitrary")),
        name="matmul_" + mode,
    )(*args)


def _mm_nt_kernel(a_ref, b_ref, o_ref):
    o_ref[...] = lax.dot_general(a_ref[...], b_ref[...], _NT,
                                 preferred_element_type=F32).astype(o_ref.dtype)


def matmul_nt_blocked(a, b, bn):
    m, k = a.shape
    n = b.shape[0]
    bm = _pick(m, MM_BN, 8)
    return pl.pallas_call(
        _mm_nt_kernel,
        grid=(m // bm, n // bn),
        in_specs=[pl.BlockSpec((bm, k), lambda i, j: (i, 0)),
                  pl.BlockSpec((bn, k), lambda i, j: (j, 0))],
        out_specs=pl.BlockSpec((None, bm, bn), lambda i, j: (j, i, 0)),
        out_shape=jax.ShapeDtypeStruct((n // bn, m, bn), BF16),
        compiler_params=_params(("arbitrary", "arbitrary")),
        name="matmul_nt",
    )(a, b)


def _softmax_block(s_t, off, vt, m_ref, acc_ref, c):
    m_prev = m_ref[c]
    m_new = jnp.maximum(m_prev, jnp.max(s_t, axis=0, keepdims=True) - off)
    alpha = jnp.exp2(m_prev - m_new)
    p_t = jnp.exp2(s_t - (m_new + off))
    acc_ref[c] = alpha * acc_ref[c] + jnp.dot(vt, p_t.astype(BF16), preferred_element_type=F32)
    m_ref[c] = m_new


def _values_with_ones(vt_ref, kb, t):
    ones_rows = (lax.broadcasted_iota(jnp.int32, (ONES_ROWS, t), 0) == 0).astype(BF16)
    return jnp.concatenate([vt_ref[kb], ones_rows], axis=0)


def _pipelined_key_loop(n_full, n_diag, scores, consume, s_a, s_b):
    assert n_diag in (1, 2)
    scores(0, s_a)

    def two_blocks(kb):
        scores(kb + 1, s_b)
        consume(s_a, kb, None)
        scores(kb + 2, s_a)
        consume(s_b, kb + 1, None)

    done = 0
    for unroll in KEY_LOOP_UNROLLS:
        count = (n_full - done) // unroll

        def unrolled(j, carry, unroll=unroll, done=done):
            for u in range(0, unroll, 2):
                two_blocks(done + unroll * j + u)
            return carry

        lax.fori_loop(0, count, unrolled, 0)
        done = done + unroll * count

    if n_diag == 2:
        scores(n_full + 1, s_b)
        consume(s_a, n_full, 0)
        consume(s_b, n_full + 1, 1)
        return

    @pl.when(n_full % 2 == 1)
    def _():
        scores(n_full, s_b)
        consume(s_a, n_full - 1, None)
        consume(s_b, n_full, 0)

    @pl.when(n_full % 2 == 0)
    def _():
        consume(s_a, n_full, 0)


def _init_stats(acc_ref, m_ref):
    acc_ref[...] = jnp.zeros_like(acc_ref)
    m_ref[...] = jnp.full_like(m_ref, NEG_INF)


def _normalised(acc_ref, c):
    acc = acc_ref[c]
    return acc[:HEAD_V_DIM] / acc[HEAD_V_DIM:HEAD_V_DIM + 1]


def _split3(x):
    hi = x.astype(BF16).astype(F32)
    r1 = x - hi
    mid = r1.astype(BF16).astype(F32)
    lo = (r1 - mid).astype(BF16).astype(F32)
    return hi, mid, lo


def _by_lane(lane, columns):
    out = jnp.zeros(lane.shape, F32)
    for i, col in enumerate(columns):
        out = jnp.where(lane == i, col, out)
    return out.astype(BF16)


def _diagonal_terms(diag_ref, slope):
    n_diag, t, tq = diag_ref.shape
    key = lax.broadcasted_iota(jnp.int32, (t, tq), 0)
    qry = lax.broadcasted_iota(jnp.int32, (t, tq), 1)
    for d in range(n_diag):
        allowed = ((key + d * t) // CHUNK) <= (qry // CHUNK)
        if slope is None:
            term = jnp.zeros((t, tq), F32)
        else:
            term = 2.0 * slope * jnp.minimum((qry - key - d * t).astype(F32), 0.0)
        diag_ref[d] = jnp.where(allowed, term, NEG_INF)


def _diff_attn_kernel(slopes_ref, q_ref, k_ref, vt_ref, lam_ref, g_ref, o_ref,
                      s_a, s_b, diag_ref, qbias_ref, kbias_ref, acc_ref, m_ref, *, t, lam_init):
    h = pl.program_id(0)
    qb = pl.program_id(1)
    tq = q_ref.shape[0]
    n_diag = tq // t
    slope = slopes_ref[h] * LOG2_E

    @pl.when(qb == 0)
    def _():
        lane_q = lax.broadcasted_iota(jnp.int32, (tq, LANES), 1)
        pos_q = lax.broadcasted_iota(jnp.int32, (tq, LANES), 0)
        slope_v = jnp.full((tq, LANES), slope, F32)
        s_hi, s_mid, s_lo = _split3(slope_v)
        a_hi, a_mid, a_lo = _split3(slope_v * pos_q.astype(F32))
        qbias_ref[...] = _by_lane(lane_q, [s_hi, s_hi, s_mid, s_mid, s_lo, s_lo, -a_hi, -a_mid, -a_lo])
        lane_k = lax.broadcasted_iota(jnp.int32, (t, LANES), 1)
        pos_k = lax.broadcasted_iota(jnp.int32, (t, LANES), 0)
        one = jnp.ones((t, LANES), F32)
        j_even = (pos_k & ~1).astype(F32)
        j_odd = (pos_k & 1).astype(F32)
        kbias_ref[...] = _by_lane(lane_k, [j_even, j_odd, j_even, j_odd, j_even, j_odd, one, one, one])
        _diagonal_terms(diag_ref, slope)

    q = (q_ref[...].astype(F32) * (DIFF_QK_DIM ** -0.5 * LOG2_E)).astype(BF16)
    lane = lax.broadcasted_iota(jnp.int32, q.shape, 1)
    zero = jnp.zeros_like(q)
    q_maps = (jnp.where(lane < DIFF_QK_DIM, q, zero), jnp.where(lane >= DIFF_QK_DIM, q, zero))
    q_aug = tuple(jnp.concatenate([qm, qbias_ref[...]], axis=1) for qm in q_maps)
    _init_stats(acc_ref, m_ref)

    def scores(kb, s_ref):
        k = jnp.concatenate([k_ref[pl.ds(pl.multiple_of(kb * t, t), t), :], kbias_ref[...]], axis=1)
        for c in range(2):
            s_ref[c] = lax.dot_general(k, q_aug[c], _NT, preferred_element_type=F32)

    def consume(s_ref, kb, diag):
        vt = _values_with_ones(vt_ref, kb, t)
        off = slope * ((n_diag * qb - kb) * t).astype(F32)
        for c in range(2):
            s_t = s_ref[c] if diag is None else s_ref[c] + diag_ref[diag]
            _softmax_block(s_t, off, vt, m_ref, acc_ref, c)

    _pipelined_key_loop(n_diag * qb, n_diag, scores, consume, s_a, s_b)

    lam_rows = lam_ref[...]
    lam = (jnp.exp(jnp.sum(lam_rows[0:1] * lam_rows[1:2], axis=1, keepdims=True))
           - jnp.exp(jnp.sum(lam_rows[2:3] * lam_rows[3:4], axis=1, keepdims=True)) + lam_init)
    d_t = _normalised(acc_ref, 0) - lam * _normalised(acc_ref, 1)
    ms = jnp.mean(d_t * d_t, axis=0, keepdims=True)
    y_t = d_t * lax.rsqrt(ms + RMS_EPS) * g_ref[...] * (1.0 - lam_init)
    o_ref[...] = y_t.T.astype(o_ref.dtype)


def diff_attention(proj, vt, lam_rows, norm_g, layer, n_heads):
    s = proj.shape[0]
    t = vt.shape[2]
    tq = ATTN_Q_BLOCKS * t
    lam_init = 0.8 - 0.6 * math.exp(-0.3 * layer)
    slopes = 2.0 ** (-8.0 * jnp.arange(1, n_heads + 1, dtype=F32) / n_heads)
    grid_spec = pltpu.PrefetchScalarGridSpec(
        num_scalar_prefetch=1,
        grid=(n_heads, s // tq),
        in_specs=[pl.BlockSpec((tq, LANES), lambda h, i, sl: (i, h)),
                  pl.BlockSpec((s, LANES), lambda h, i, sl: (0, n_heads + h)),
                  pl.BlockSpec((s // t, HEAD_V_DIM, t), lambda h, i, sl: (0, h, 0)),
                  pl.BlockSpec((4, DIFF_QK_DIM), lambda h, i, sl: (0, 0)),
                  pl.BlockSpec((HEAD_V_DIM, 1), lambda h, i, sl: (0, 0))],
        out_specs=pl.BlockSpec((tq, HEAD_V_DIM), lambda h, i, sl: (i, h)),
        scratch_shapes=[pltpu.VMEM((2, t, tq), F32),
                        pltpu.VMEM((2, t, tq), F32),
                        pltpu.VMEM((ATTN_Q_BLOCKS, t, tq), F32),
                        pltpu.VMEM((tq, LANES), BF16),
                        pltpu.VMEM((t, LANES), BF16),
                        pltpu.VMEM((2, HEAD_V_DIM + ONES_ROWS, tq), F32),
                        pltpu.VMEM((2, 1, tq), F32)])
    return pl.pallas_call(
        functools.partial(_diff_attn_kernel, t=t, lam_init=lam_init),
        grid_spec=grid_spec,
        out_shape=jax.ShapeDtypeStruct((s, n_heads * HEAD_V_DIM), BF16),
        compiler_params=_params(("arbitrary", "arbitrary")),
        name="diff_attention",
    )(slopes, proj, proj, vt, lam_rows, norm_g.reshape(HEAD_V_DIM, 1).astype(F32))


def _mla_attn_kernel(q_ref, kn_ref, kpe_ref, vt_ref, o_ref, s_a, s_b, diag_ref, acc_ref, m_ref, *, t):
    qb = pl.program_id(1)
    n_diag = q_ref.shape[0] // t

    @pl.when(qb == 0)
    def _():
        _diagonal_terms(diag_ref, None)

    scale = (MLA_NOPE_DIM + MLA_ROPE_DIM) ** -0.5 * LOG2_E
    q = (q_ref[...].astype(F32) * scale).astype(BF16)
    _init_stats(acc_ref, m_ref)

    def scores(kb, s_ref):
        k0 = pl.multiple_of(kb * t, t)
        k = jnp.concatenate([kn_ref[pl.ds(k0, t), :], kpe_ref[pl.ds(k0, t), :]], axis=1)
        s_ref[0] = lax.dot_general(k, q, _NT, preferred_element_type=F32)

    def consume(s_ref, kb, diag):
        s_t = s_ref[0] if diag is None else s_ref[0] + diag_ref[diag]
        _softmax_block(s_t, 0.0, _values_with_ones(vt_ref, kb, t), m_ref, acc_ref, 0)

    _pipelined_key_loop(n_diag * qb, n_diag, scores, consume, s_a, s_b)
    o_ref[...] = _normalised(acc_ref, 0).T.astype(o_ref.dtype)


def mla_attention(q, kn, k_pe, vt, n_heads):
    s = q.shape[0]
    t = vt.shape[2]
    tq = ATTN_Q_BLOCKS * t
    return pl.pallas_call(
        functools.partial(_mla_attn_kernel, t=t),
        grid=(n_heads, s // tq),
        in_specs=[pl.BlockSpec((tq, MLA_QK_PAD), lambda h, i: (i, h)),
                  pl.BlockSpec((s, LANES), lambda h, i: (0, h)),
                  pl.BlockSpec((s, LANES), lambda h, i: (0, 0)),
                  pl.BlockSpec((s // t, HEAD_V_DIM, t), lambda h, i: (0, h, 0))],
        out_specs=pl.BlockSpec((tq, HEAD_V_DIM), lambda h, i: (i, h)),
        out_shape=jax.ShapeDtypeStruct((s, n_heads * HEAD_V_DIM), BF16),
        scratch_shapes=[pltpu.VMEM((1, t, tq), F32),
                        pltpu.VMEM((1, t, tq), F32),
                        pltpu.VMEM((ATTN_Q_BLOCKS, t, tq), F32),
                        pltpu.VMEM((1, HEAD_V_DIM + ONES_ROWS, tq), F32),
                        pltpu.VMEM((1, 1, tq), F32)],
        compiler_params=_params(("arbitrary", "arbitrary")),
        name="mla_attention",
    )(q, kn, k_pe, vt)


def _router_kernel(h_ref, g_ref, w_ref, b_ref, xn_ref, meta_i_ref, meta_f_ref, cnt_ref, carry_ref,
                   *, n_exp):
    tb = h_ref.shape[0]

    @pl.when(pl.program_id(0) == 0)
    def _():
        carry_ref[...] = jnp.zeros_like(carry_ref)

    x = h_ref[...]
    ms = jnp.mean(x * x, axis=-1, keepdims=True)
    xn = x * lax.rsqrt(ms + RMS_EPS) * g_ref[...]
    xn_ref[...] = xn.astype(xn_ref.dtype)
    logits = jnp.dot(xn, w_ref[...], preferred_element_type=F32,
                     precision=lax.Precision.HIGHEST) + b_ref[...]
    lane = lax.broadcasted_iota(jnp.int32, logits.shape, 1)
    lane_f = lane.astype(F32)
    big = jnp.asarray(4.0 * LANES, F32)

    def first_lane(cond):
        return jnp.min(jnp.where(cond, lane_f, big), axis=1, keepdims=True)

    is_group = (lane >= n_exp) & (lane < n_exp + N_GROUPS)
    g_logit = jnp.where(is_group, logits, NEG_INF)
    g_max = jnp.max(g_logit, axis=1, keepdims=True)
    g_sum = jnp.sum(jnp.where(is_group, jnp.exp(g_logit - g_max), 0.0), axis=1, keepdims=True)
    g_val = 1.0 / g_sum
    g_idx = first_lane(is_group & (g_logit == g_max)) - n_exp
    in_group = (lane < n_exp) & ((lane // EXPERTS_PER_GROUP).astype(F32) == g_idx)
    e_logit = jnp.where(in_group, logits, NEG_INF)
    top1 = jnp.max(e_logit, axis=1, keepdims=True)
    idx1 = first_lane(in_group & (e_logit == top1))
    e_logit2 = jnp.where(lane_f == idx1, NEG_INF, e_logit)
    top2 = jnp.max(e_logit2, axis=1, keepdims=True)
    idx2 = first_lane(in_group & (lane_f != idx1) & (e_logit2 == top2))
    ratio = jnp.exp(top2 - top1)
    gate1 = g_val / (1.0 + ratio)
    gate2 = g_val * ratio / (1.0 + ratio)

    sel1 = lane_f == idx1
    sel2 = lane_f == idx2
    onehot = jnp.where(sel1 | sel2, 1.0, 0.0)
    r = lax.broadcasted_iota(jnp.int32, (tb, tb), 0)
    c = lax.broadcasted_iota(jnp.int32, (tb, tb), 1)
    before = jnp.where(c < r, 1.0, 0.0).astype(BF16)
    pos = jnp.dot(before, onehot.astype(BF16), preferred_element_type=F32) + carry_ref[...]
    rank1 = jnp.sum(jnp.where(sel1, pos, 0.0), axis=1, keepdims=True)
    rank2 = jnp.sum(jnp.where(sel2, pos, 0.0), axis=1, keepdims=True)
    carry_ref[...] = carry_ref[...] + jnp.sum(onehot, axis=0, keepdims=True)
    cnt_ref[...] = carry_ref[...]

    def lanes4(a, b, c4, d):
        return jnp.where(lane == 0, a, jnp.where(lane == 1, b, jnp.where(lane == 2, c4, d)))

    meta_i_ref[...] = lanes4(idx1, idx2, rank1, rank2).astype(jnp.int32)
    meta_f_ref[...] = lanes4(gate1, gate2, 0.0, 0.0)


def moe_router(h, gain, w_group, b_group, w_expert, b_expert):
    s, d = h.shape
    n_exp = w_expert.shape[1]
    assert n_exp == N_GROUPS * EXPERTS_PER_GROUP and n_exp + N_GROUPS <= LANES
    pad = LANES - n_exp - N_GROUPS
    w_cat = jnp.concatenate([w_expert, w_group, jnp.zeros((d, pad), F32)], axis=1).astype(F32)
    b_cat = jnp.concatenate([b_expert, b_group, jnp.zeros((pad,), F32)]).reshape(1, LANES).astype(F32)
    tb = _pick(s, ROUTER_ROWS, 8)
    return pl.pallas_call(
        functools.partial(_router_kernel, n_exp=n_exp),
        grid=(s // tb,),
        in_specs=[pl.BlockSpec((tb, d), lambda i: (i, 0)),
                  pl.BlockSpec((1, d), lambda i: (0, 0)),
                  pl.BlockSpec((d, LANES), lambda i: (0, 0)),
                  pl.BlockSpec((1, LANES), lambda i: (0, 0))],
        out_specs=[pl.BlockSpec((tb, d), lambda i: (i, 0)),
                   pl.BlockSpec((tb, LANES), lambda i: (i, 0)),
                   pl.BlockSpec((tb, LANES), lambda i: (i, 0)),
                   pl.BlockSpec((1, LANES), lambda i: (0, 0))],
        out_shape=[jax.ShapeDtypeStruct((s, d), F32),
                   jax.ShapeDtypeStruct((s, LANES), jnp.int32),
                   jax.ShapeDtypeStruct((s, LANES), F32),
                   jax.ShapeDtypeStruct((1, LANES), F32)],
        scratch_shapes=[pltpu.VMEM((1, LANES), F32)],
        compiler_params=_params(("arbitrary",)),
        name="moe_router",
    )(h, gain.reshape(1, d).astype(F32), w_cat, b_cat)


def _scatter_rows_kernel(d0_ref, d1_ref, pad_lo_ref, pad_hi_ref, x_ref, xs_ref, zero_ref, sems):
    tb = x_ref.shape[0]
    base = pl.program_id(0) * tb

    def copies(t):
        row = x_ref.at[pl.ds(t, 1), :]
        return (pltpu.make_async_copy(row, xs_ref.at[pl.ds(d0_ref[base + t], 1), :], sems.at[0]),
                pltpu.make_async_copy(row, xs_ref.at[pl.ds(d1_ref[base + t], 1), :], sems.at[1]))

    def start(t, carry):
        for cp in copies(t):
            cp.start()
        return carry

    def wait(t, carry):
        for cp in copies(t):
            cp.wait()
        return carry

    lax.fori_loop(0, tb, start, 0)
    lax.fori_loop(0, tb, wait, 0)

    @pl.when(pl.program_id(0) == pl.num_programs(0) - 1)
    def _():
        zero_ref[...] = jnp.zeros_like(zero_ref)

        def pad_copy(r):
            return pltpu.make_async_copy(zero_ref.at[pl.ds(0, 1), :], xs_ref.at[pl.ds(r, 1), :], sems.at[2])

        def for_pad_rows(fn):
            def per_expert(e, carry):
                lax.fori_loop(pad_lo_ref[e], pad_hi_ref[e], fn, 0)
                return carry
            lax.fori_loop(0, pad_lo_ref.shape[0], per_expert, 0)

        def start_pad(r, carry):
            pad_copy(r).start()
            return carry

        def wait_pad(r, carry):
            pad_copy(r).wait()
            return carry

        for_pad_rows(start_pad)
        for_pad_rows(wait_pad)


def scatter_rows(x, dest0, dest1, pad_lo, pad_hi, n_rows):
    n, d = x.shape
    tb = _pick(n, MOVE_ROWS, 8)
    grid_spec = pltpu.PrefetchScalarGridSpec(
        num_scalar_prefetch=4,
        grid=(n // tb,),
        in_specs=[pl.BlockSpec((tb, d), lambda i, a, b, lo, hi: (i, 0))],
        out_specs=pl.BlockSpec(memory_space=pl.ANY),
        scratch_shapes=[pltpu.VMEM((8, d), x.dtype),
                        pltpu.SemaphoreType.DMA((3,))])
    return pl.pallas_call(
        _scatter_rows_kernel,
        grid_spec=grid_spec,
        out_shape=jax.ShapeDtypeStruct((n_rows, d), x.dtype),
        compiler_params=_params(("arbitrary",)),
        name="moe_scatter_rows",
    )(dest0, dest1, pad_lo, pad_hi, x)


def _expert_up_kernel(be_ref, nv_ref, x_ref, wg_ref, wu_ref, o_ref):
    b = pl.program_id(0)

    @pl.when(b < nv_ref[0])
    def _():
        x = x_ref[...].astype(BF16)
        gate = jnp.dot(x, wg_ref[...], preferred_element_type=F32)
        up = jnp.dot(x, wu_ref[...], preferred_element_type=F32)
        o_ref[...] = (jax.nn.silu(gate) * up).astype(o_ref.dtype)

    @pl.when(b >= nv_ref[0])
    def _():
        o_ref[...] = jnp.zeros_like(o_ref)


def _expert_down_kernel(be_ref, nv_ref, x_ref, wd_ref, o_ref):
    b = pl.program_id(0)

    @pl.when(b < nv_ref[0])
    def _():
        o_ref[...] = jnp.dot(x_ref[...], wd_ref[...], preferred_element_type=F32).astype(o_ref.dtype)

    @pl.when(b >= nv_ref[0])
    def _():
        o_ref[...] = jnp.zeros_like(o_ref)


def expert_ffn(xs, blk_e, n_valid, w_gate, w_up, w_down, layer):
    n_rows, d = xs.shape
    de = w_gate.shape[3]
    rb = MOE_ROWS
    n_blk = n_rows // rb
    up_spec = pltpu.PrefetchScalarGridSpec(
        num_scalar_prefetch=2,
        grid=(n_blk,),
        in_specs=[pl.BlockSpec((rb, d), lambda b, be, nv: (jnp.minimum(b, nv[0] - 1), 0)),
                  pl.BlockSpec((None, None, d, de), lambda b, be, nv: (layer, be[b], 0, 0)),
                  pl.BlockSpec((None, None, d, de), lambda b, be, nv: (layer, be[b], 0, 0))],
        out_specs=pl.BlockSpec((rb, de), lambda b, be, nv: (b, 0)))
    hid = pl.pallas_call(
        _expert_up_kernel,
        grid_spec=up_spec,
        out_shape=jax.ShapeDtypeStruct((n_rows, de), BF16),
        compiler_params=_params(("arbitrary",)),
        name="moe_expert_up",
    )(blk_e, n_valid, xs, w_gate, w_up)
    down_spec = pltpu.PrefetchScalarGridSpec(
        num_scalar_prefetch=2,
        grid=(n_blk,),
        in_specs=[pl.BlockSpec((rb, de), lambda b, be, nv: (b, 0)),
                  pl.BlockSpec((None, None, de, d), lambda b, be, nv: (layer, be[b], 0, 0))],
        out_specs=pl.BlockSpec((rb, d), lambda b, be, nv: (b, 0)))
    return pl.pallas_call(
        _expert_down_kernel,
        grid_spec=down_spec,
        out_shape=jax.ShapeDtypeStruct((n_rows, d), F32),
        compiler_params=_params(("arbitrary",)),
        name="moe_expert_down",
    )(blk_e, n_valid, hid, w_down)


def _combine_kernel(d0_ref, d1_ref, h_ref, gate_ref, gain_ref, y_ref, o_ref, on_ref, buf_ref, sems):
    tb = h_ref.shape[0]
    base = pl.program_id(0) * tb

    def copies(t):
        return (pltpu.make_async_copy(y_ref.at[pl.ds(d0_ref[base + t], 1), :],
                                      buf_ref.at[0, pl.ds(t, 1), :], sems.at[0]),
                pltpu.make_async_copy(y_ref.at[pl.ds(d1_ref[base + t], 1), :],
                                      buf_ref.at[1, pl.ds(t, 1), :], sems.at[1]))

    def start(t, carry):
        for cp in copies(t):
            cp.start()
        return carry

    def wait(t, carry):
        for cp in copies(t):
            cp.wait()
        return carry

    lax.fori_loop(0, tb, start, 0)
    lax.fori_loop(0, tb, wait, 0)
    gates = gate_ref[...]
    out = h_ref[...] + gates[:, 0:1] * buf_ref[0] + gates[:, 1:2] * buf_ref[1]
    o_ref[...] = out
    ms = jnp.mean(out * out, axis=-1, keepdims=True)
    on_ref[...] = (out * lax.rsqrt(ms + RMS_EPS) * gain_ref[...]).astype(on_ref.dtype)


def combine_rows(h, y, dest0, dest1, gates, next_gain):
    n, d = h.shape
    tb = _pick(n, MOVE_ROWS, 8)
    grid_spec = pltpu.PrefetchScalarGridSpec(
        num_scalar_prefetch=2,
        grid=(n // tb,),
        in_specs=[pl.BlockSpec((tb, d), lambda i, a, b: (i, 0)),
                  pl.BlockSpec((tb, LANES), lambda i, a, b: (i, 0)),
                  pl.BlockSpec((1, d), lambda i, a, b: (0, 0)),
                  pl.BlockSpec(memory_space=pl.ANY)],
        out_specs=[pl.BlockSpec((tb, d), lambda i, a, b: (i, 0)),
                   pl.BlockSpec((tb, d), lambda i, a, b: (i, 0))],
        scratch_shapes=[pltpu.VMEM((2, tb, d), y.dtype),
                        pltpu.SemaphoreType.DMA((2,))])
    return pl.pallas_call(
        _combine_kernel,
        grid_spec=grid_spec,
        out_shape=[jax.ShapeDtypeStruct(h.shape, F32), jax.ShapeDtypeStruct(h.shape, BF16)],
        compiler_params=_params(("arbitrary",)),
        name="moe_combine_rows",
    )(dest0, dest1, h, gates, next_gain.reshape(1, d).astype(F32), y)


def hierarchical_moe(h, gain, w_group, b_group, w_expert, b_expert, w_gate, w_up, w_down, layer, next_gain):
    s, d = h.shape
    n_exp = w_expert.shape[1]
    rb = MOE_ROWS
    xn, meta_i, meta_f, cnt = moe_router(h, gain, w_group, b_group, w_expert, b_expert)
    counts = cnt[0, :n_exp].astype(jnp.int32)
    padded = (counts + rb - 1) // rb * rb
    pend = jnp.cumsum(padded)
    pstart = pend - padded
    eid = meta_i[:, :TOP_K]
    dest = pstart[eid] + meta_i[:, TOP_K:2 * TOP_K]
    n_rows = (s * TOP_K + n_exp * (rb - 1) + rb - 1) // rb * rb
    n_blk = n_rows // rb
    blk_start = jnp.arange(n_blk, dtype=jnp.int32) * rb
    blk_e = jnp.minimum(jnp.sum((pend[None, :] <= blk_start[:, None]).astype(jnp.int32), axis=1),
                        n_exp - 1).astype(jnp.int32)
    n_valid = (pend[-1:] // rb).astype(jnp.int32)
    dest0, dest1 = dest[:, 0], dest[:, 1]
    pad_lo = jnp.concatenate([pstart + counts, pend[-1:]]).astype(jnp.int32)
    pad_hi = jnp.concatenate([pend, jnp.full((1,), n_rows, pend.dtype)]).astype(jnp.int32)
    xs = scatter_rows(xn, dest0, dest1, pad_lo, pad_hi, n_rows)
    y = expert_ffn(xs, blk_e, n_valid, w_gate, w_up, w_down, layer)
    return combine_rows(h, y, dest0, dest1, meta_f, next_gain)


def _rope_tables(s):
    half = MLA_ROPE_DIM // 2
    inv_freq = 1.0 / (ROPE_BASE ** (jnp.arange(0, MLA_ROPE_DIM, 2, dtype=F32) / MLA_ROPE_DIM))
    ang = jnp.arange(s, dtype=F32)[:, None] * inv_freq[None, :]
    cos, sin = jnp.cos(ang), jnp.sin(ang)
    z = jnp.zeros((s, half), F32)
    return (jnp.concatenate([cos, z, cos, z], axis=1), jnp.concatenate([-sin, z, sin, z], axis=1))


def _spread_rope_cols(w):
    half = MLA_ROPE_DIM // 2
    z = jnp.zeros((w.shape[0], half), w.dtype)
    return jnp.concatenate([w[:, :half], z, w[:, half:], z], axis=1)


def _pad_uq(w_uq, n_heads):
    r = w_uq.shape[0]
    w = w_uq.reshape(r, n_heads, MLA_NOPE_DIM + MLA_ROPE_DIM)
    half = MLA_ROPE_DIM // 2
    z = jnp.zeros((r, n_heads, half), w.dtype)
    nope, pe = w[..., :MLA_NOPE_DIM], w[..., MLA_NOPE_DIM:]
    w = jnp.concatenate([nope, pe[..., :half], z, pe[..., half:], z], axis=-1)
    return w.reshape(r, n_heads * MLA_QK_PAD)


def kernel(x, p, ln_mix_g, w_in, lam_q1, lam_k1, lam_q2, lam_k2, diff_norm_g, mla_qa_norm_g, w_uq, mla_kva_norm_g, w_ukv, w_out, ln_ffn_g, w_group_router, b_group_router, w_expert_router, b_expert_router, w_exp_gate, w_exp_up, w_exp_down, ln_ple_g, w_ple_gate, w_ple_proj, ln_final_g):
    batch, s, d = x.shape
    assert batch == 1
    depth = w_in.shape[0]
    diff_width = d // 2
    n_diff = diff_width // HEAD_V_DIM
    n_mla = (d // 2) // HEAD_V_DIM
    q_rank = mla_qa_norm_g.shape[1]
    kv_rank = mla_kva_norm_g.shape[1]
    v_start = 2 * diff_width
    cq_start = 3 * diff_width
    rope_start = cq_start + q_rank + kv_rank
    assert w_in.shape[2] == rope_start + MLA_ROPE_DIM
    t = _pick(s, ATTN_TILE, CHUNK)
    cos_t, sin_t = _rope_tables(s)
    w_gate_all = w_exp_gate.astype(BF16)
    w_up_all = w_exp_up.astype(BF16)
    w_down_all = w_exp_down.astype(BF16)

    h = x.reshape(s, d)
    for i in range(depth):
        w_in_i = w_in[i]
        w_main = jnp.concatenate([w_in_i[:, :v_start], w_in_i[:, cq_start:rope_start]], axis=1).astype(BF16)
        w_v_t = w_in_i[:, v_start:cq_start].T.astype(BF16)
        w_rope = _spread_rope_cols(w_in_i[:, rope_start:]).astype(BF16)
        hn = rmsnorm(h, ln_mix_g[i])
        proj = matmul(hn, w_main)
        vt_diff = matmul_nt_blocked(w_v_t, hn, t)
        k_pe = matmul(hn, w_rope, mode="rope_all", cos=cos_t, sin=sin_t)
        lam_rows = jnp.stack([lam_q1[i], lam_k1[i], lam_q2[i], lam_k2[i]]).astype(F32)
        y_diff = diff_attention(proj, vt_diff, lam_rows, diff_norm_g[i], i, n_diff)
        cqn = rmsnorm(proj, mla_qa_norm_g[i], col_start=v_start, width=q_rank)
        ckvn = rmsnorm(proj, mla_kva_norm_g[i], col_start=v_start + q_rank, width=kv_rank)
        q_mla = matmul(cqn, _pad_uq(w_uq[i], n_mla).astype(BF16), mode="rope_upper", cos=cos_t, sin=sin_t)
        w_ukv_i = w_ukv[i].reshape(kv_rank, n_mla, MLA_NOPE_DIM + HEAD_V_DIM)
        w_kn = w_ukv_i[:, :, :MLA_NOPE_DIM].reshape(kv_rank, n_mla * MLA_NOPE_DIM).astype(BF16)
        w_vm_t = w_ukv_i[:, :, MLA_NOPE_DIM:].reshape(kv_rank, n_mla * HEAD_V_DIM).T.astype(BF16)
        kn = matmul(ckvn, w_kn)
        vt_mla = matmul_nt_blocked(w_vm_t, ckvn, t)
        y_mla = mla_attention(q_mla, kn, k_pe, vt_mla, n_mla)
        h = matmul(y_diff, w_out[i].astype(BF16), mode="residual2", out_dtype=F32, res=h, x2=y_mla)
        h, hn = hierarchical_moe(h, ln_ffn_g[i], w_group_router[i], b_group_router[i],
                                 w_expert_router[i], b_expert_router[i],
                                 w_gate_all, w_up_all, w_down_all, i, ln_ple_g[i])
        h = matmul(hn, w_ple_gate[i].astype(BF16), mode="ple", out_dtype=F32, res=h,
                   p=p[i].reshape(s, -1), wp=w_ple_proj[i].astype(BF16))
    return rmsnorm(h, ln_final_g, out_dtype=x.dtype).reshape(batch, s, d)
```
